```python
import math
import jax
import jax.numpy as jnp
from jax import lax
import numpy as np


D_MODEL = 1024
BATCH = 8
SEQ = 2048
DEPTH = 2

GRID_W = 64
CTX_LEN = 256
EPS = 1e-6
ROPE_BASE = 10000.0
ROPE_DIM = 64
Q_BLOCK = 128
MASK_VALUE = -1e30

A_HEADS = 4
A_QK = 64
A_V = 2 * A_QK
B_HEADS = 4
B_Q_LORA = 256
B_KV_LORA = 128
B_NOPE = 128
B_ROPE = 64
B_V = 128
C_HEADS = 4
C_DK = 128
C_DV = 128
C_CHUNK = 64
D_HEADS = 8
D_KV_HEADS = 2
D_HD = 64
WINDOW = 128
N_BRANCH = 4
BR_W = 512
PEER_HEADS = 8
N_KEYS = 128
N_EXPERTS = N_KEYS * N_KEYS
PEER_DK = 128
PEER_TOPK = 16
PEER_TOK_BLOCK = 128

IN_SIZES = (A_HEADS * A_QK, A_HEADS * A_QK, A_HEADS * A_QK, A_HEADS * A_QK, A_HEADS * A_V,
            B_Q_LORA, B_KV_LORA, B_ROPE,
            C_HEADS * C_DK, C_HEADS * C_DK, C_HEADS * C_DK, C_HEADS * C_DV, C_HEADS * C_DV,
            D_HEADS * D_HD, D_KV_HEADS * D_HD, D_KV_HEADS * D_HD,
            N_BRANCH * D_MODEL)
P_IN = sum(IN_SIZES)
CTX_KV_ONLY = (False, False, True, True, True, False, True, True, False, True, True, True, False, False, True, True, False)

kernel_name = 'hybrid_gated_mixers_peer_dit'


def rmsnorm(x, g):
    xf = x.astype(jnp.float32)
    y = xf * lax.rsqrt(jnp.mean(xf * xf, axis=-1, keepdims=True) + EPS)
    return (y * g.astype(jnp.float32)).astype(x.dtype)


def modulate(h, shift, scale):
    return h * (1.0 + scale) + shift


def project(h, w, keep):
    outs, off = [], 0
    if keep is None:
        p = h @ w
        for size in IN_SIZES:
            outs.append(p[..., off:off + size])
            off += size
    else:
        for size, k in zip(IN_SIZES, keep):
            outs.append(h @ w[:, off:off + size] if k else None)
            off += size
    return outs


def axial_angles(n, dim):
    rows = n // GRID_W
    r = jnp.repeat(jnp.arange(rows, dtype=jnp.float32), GRID_W)
    col = jnp.tile(jnp.arange(GRID_W, dtype=jnp.float32), rows)
    m = dim // 2
    freqs = ROPE_BASE ** (-2.0 * jnp.arange(m // 2, dtype=jnp.float32) / m)
    return jnp.concatenate([r[:, None] * freqs, col[:, None] * freqs], axis=-1)


def apply_rope(x, ang):
    B, n, H, dim = x.shape
    m = dim // 2
    xs = x.reshape(B, n, H, 2, 2, m // 2)
    a = ang.reshape(n, 1, 2, m // 2)
    cos = jnp.cos(a).astype(x.dtype)
    sin = jnp.sin(a).astype(x.dtype)
    x1, x2 = xs[..., 0, :], xs[..., 1, :]
    out = jnp.stack([x1 * cos - x2 * sin, x1 * sin + x2 * cos], axis=-2)
    return out.reshape(B, n, H, dim)


def over_query_blocks(f, *qs):
    B, n = qs[0].shape[:2]
    nb = n // Q_BLOCK
    split = lambda t: jnp.moveaxis(t.reshape((B, nb, Q_BLOCK) + t.shape[2:]), 1, 0)
    out = lax.map(lambda args: f(*args), tuple(split(t) for t in qs))
    return jnp.moveaxis(out, 0, 1).reshape((B, n) + out.shape[3:])


def attn_core(q, k, v, scale):
    p = jax.nn.softmax((jnp.einsum('bqhd,bkhd->bhqk', q, k) * scale).astype(jnp.float32), axis=-1)
    return jnp.einsum('bhqk,bkhd->bqhd', p.astype(v.dtype), v)


def diff_core(q1, q2, k1, k2, v, lam):
    scale = A_QK ** -0.5
    p1 = jax.nn.softmax((jnp.einsum('bqhd,bkhd->bhqk', q1, k1) * scale).astype(jnp.float32), axis=-1)
    p2 = jax.nn.softmax((jnp.einsum('bqhd,bkhd->bhqk', q2, k2) * scale).astype(jnp.float32), axis=-1)
    return jnp.einsum('bhqk,bkhd->bqhd', (p1 - lam * p2).astype(v.dtype), v)


def mixer_diff(px, pc, lq1, lk1, lq2, lk2, norm_g, layer, ang, ctx_out):
    heads = lambda t, d: t.reshape(t.shape[0], t.shape[1], A_HEADS, d)
    q1, q2, k1, k2 = [apply_rope(heads(t, A_QK), ang) for t in px[:4]]
    v = heads(px[4], A_V)
    k1c, k2c, vc = heads(pc[2], A_QK), heads(pc[3], A_QK), heads(pc[4], A_V)
    lam_init = 0.8 - 0.6 * math.exp(-0.3 * layer)
    f32 = jnp.float32
    lam = (jnp.exp(jnp.sum(lq1.astype(f32) * lk1.astype(f32)))
           - jnp.exp(jnp.sum(lq2.astype(f32) * lk2.astype(f32))) + lam_init)
    K1 = jnp.concatenate([k1, k1c], axis=1)
    K2 = jnp.concatenate([k2, k2c], axis=1)
    V = jnp.concatenate([v, vc], axis=1)
    ox = over_query_blocks(lambda a, b: diff_core(a, b, K1, K2, V, lam), q1, q2)
    post = lambda o: (rmsnorm(o, norm_g) * (1.0 - lam_init)).reshape(o.shape[0], o.shape[1], -1)
    oc = None
    if ctx_out:
        oc = post(diff_core(heads(pc[0], A_QK), heads(pc[1], A_QK), k1c, k2c, vc, lam))
    return post(ox), oc


def mixer_mla(px, pc, qn_g, kvn_g, w_uq, w_ukv, ang, ctx_out):
    def q_of(cq, rope):
        B, n = cq.shape[:2]
        q = (rmsnorm(cq, qn_g) @ w_uq).reshape(B, n, B_HEADS, B_NOPE + B_ROPE)
        q_rope = q[..., B_NOPE:]
        if rope:
            q_rope = apply_rope(q_rope, ang)
        return jnp.concatenate([q[..., :B_NOPE], q_rope], axis=-1)

    def kv_of(ckv, kr, rope):
        B, n = ckv.shape[:2]
        kv = (rmsnorm(ckv, kvn_g) @ w_ukv).reshape(B, n, B_HEADS, B_NOPE + B_V)
        k_rope = kr.reshape(B, n, 1, B_ROPE)
        if rope:
            k_rope = apply_rope(k_rope, ang)
        k = jnp.concatenate([kv[..., :B_NOPE], jnp.broadcast_to(k_rope, (B, n, B_HEADS, B_ROPE))], axis=-1)
        return k, kv[..., B_NOPE:]

    scale = (B_NOPE + B_ROPE) ** -0.5
    kx, vx = kv_of(px[1], px[2], True)
    kc, vc = kv_of(pc[1], pc[2], False)
    K = jnp.concatenate([kx, kc], axis=1)
    V = jnp.concatenate([vx, vc], axis=1)
    ox = over_query_blocks(lambda q: attn_core(q, K, V, scale), q_of(px[0], True))
    flat = lambda o: o.reshape(o.shape[0], o.shape[1], -1)
    oc = flat(attn_core(q_of(pc[0], False), kc, vc, scale)) if ctx_out else None
    return flat(ox), oc


def gla_scan(q, k, v, logf, s0):
    B, n, H, dk = q.shape
    nc = n // C_CHUNK
    chunks = lambda t: t.astype(jnp.float32).reshape(B, nc, C_CHUNK, H, t.shape[-1]).transpose(1, 0, 3, 2, 4)
    incl = jnp.tril(jnp.ones((C_CHUNK, C_CHUNK), dtype=bool))[:, :, None]

    def step(S, inp):
        qc, kc, vc, gc = inp
        b = jnp.cumsum(gc, axis=2)
        o_prev = jnp.einsum('bhtk,bhkv->bhtv', qc * jnp.exp(b), S)
        diff = b[:, :, :, None, :] - b[:, :, None, :, :]
        dec = jnp.where(incl, jnp.exp(jnp.where(incl, diff, 0.0)), 0.0)
        att = jnp.einsum('bhtk,bhsk,bhtsk->bhts', qc, kc, dec)
        o = o_prev + jnp.einsum('bhts,bhsv->bhtv', att, vc)
        b_end = b[:, :, -1:, :]
        S = (jnp.exp(b_end[:, :, 0, :])[..., None] * S
             + jnp.einsum('bhsk,bhsv->bhkv', kc * jnp.exp(b_end - b), vc))
        return S, o

    S, o = lax.scan(step, s0, (chunks(q), chunks(k), chunks(v), chunks(logf)))
    o = o.transpose(1, 0, 3, 2, 4).reshape(B, n, H, v.shape[-1])
    return o.astype(v.dtype), S


def gla_final_state(k, v, logf):
    b = jnp.cumsum(logf.astype(jnp.float32), axis=1)
    w = k.astype(jnp.float32) * jnp.exp(b[:, -1:] - b)
    return jnp.einsum('bnhk,bnhv->bhkv', w, v.astype(jnp.float32))


def mixer_hgrn(px, pc, lb_f, lb_b, norm_g, ctx_out):
    heads = lambda t, d: t.reshape(t.shape[0], t.shape[1], C_HEADS, d)
    flip = lambda t: jnp.flip(t, axis=1)

    def decay(z, lb):
        lbh = lb.astype(jnp.float32).reshape(C_HEADS, C_DK)
        f = lbh + (1.0 - lbh) * jax.nn.sigmoid(heads(z, C_DK).astype(jnp.float32))
        return jnp.log(f), 1.0 - f

    def readout(o, g):
        B, n = o.shape[:2]
        return (rmsnorm(o, norm_g) * jax.nn.silu(heads(g, C_DV))).reshape(B, n, C_HEADS * C_DV)

    B = px[0].shape[0]
    zeros = jnp.zeros((B, C_HEADS, C_DK, C_DV), jnp.float32)
    lf_cf, k_cf = decay(pc[1], lb_f)
    lf_cb, k_cb = decay(pc[2], lb_b)
    i_c = heads(pc[3], C_DV)
    oc = None
    if ctx_out:
        q_c = heads(pc[0], C_DK)
        o_cf, S_f = gla_scan(q_c, k_cf, i_c, lf_cf, zeros)
        o_cb, S_b = gla_scan(flip(q_c), flip(k_cb), flip(i_c), flip(lf_cb), zeros)
        oc = readout(o_cf + flip(o_cb), pc[4])
    else:
        S_f = gla_final_state(k_cf, i_c, lf_cf)
        S_b = gla_final_state(flip(k_cb), flip(i_c), flip(lf_cb))
    q_x = heads(px[0], C_DK)
    lf_xf, k_xf = decay(px[1], lb_f)
    lf_xb, k_xb = decay(px[2], lb_b)
    i_x = heads(px[3], C_DV)
    o_xf, _ = gla_scan(q_x, k_xf, i_x, lf_xf, S_f)
    o_xb, _ = gla_scan(flip(q_x), flip(k_xb), flip(i_x), flip(lf_xb), S_b)
    return readout(o_xf + flip(o_xb), px[4]), oc


def mixer_window(px, pc, sink, ang, ctx_out):
    B, n = px[0].shape[:2]
    G = D_HEADS // D_KV_HEADS
    nb = n // WINDOW
    scale = D_HD ** -0.5
    q = apply_rope(px[0].reshape(B, n, D_HEADS, D_HD), ang)
    k = apply_rope(px[1].reshape(B, n, D_KV_HEADS, D_HD), ang)
    v = px[2].reshape(B, n, D_KV_HEADS, D_HD)
    m = pc[1].shape[1]
    kc = pc[1].reshape(B, m, D_KV_HEADS, D_HD)
    vc = pc[2].reshape(B, m, D_KV_HEADS, D_HD)
    sink_b = sink.astype(jnp.float32).reshape(D_KV_HEADS, G)[:, :, None, None]
    L = 3 * WINDOW

    def band(t):
        tp = jnp.pad(t, ((0, 0), (WINDOW, WINDOW), (0, 0), (0, 0))).reshape(B, nb + 2, WINDOW, D_KV_HEADS, D_HD)
        return jnp.concatenate([tp[:, :-2], tp[:, 1:-1], tp[:, 2:]], axis=2)

    j = jnp.arange(L)
    a = jnp.arange(WINDOW)
    rel = j[None, :] - WINDOW - a[:, None]

    def block(args):
        qi, ki, vi, i = args
        kpos = i * WINDOW + j - WINDOW
        valid = (jnp.abs(rel) <= WINDOW) & ((kpos >= 0) & (kpos < n))[None, :]
        s_loc = jnp.where(valid, (jnp.einsum('bqkgd,bjkd->bkgqj', qi, ki) * scale).astype(jnp.float32), MASK_VALUE)
        s_ctx = (jnp.einsum('bqkgd,bckd->bkgqc', qi, kc) * scale).astype(jnp.float32)
        s_sink = jnp.broadcast_to(sink_b, s_ctx.shape[:-1] + (1,))
        p = jax.nn.softmax(jnp.concatenate([s_loc, s_ctx, s_sink], axis=-1), axis=-1).astype(vi.dtype)
        return (jnp.einsum('bkgqj,bjkd->bqkgd', p[..., :L], vi)
                + jnp.einsum('bkgqc,bckd->bqkgd', p[..., L:L + m], vc))

    qb = jnp.moveaxis(q.reshape(B, nb, WINDOW, D_KV_HEADS, G, D_HD), 1, 0)
    o = lax.map(block, (qb, jnp.moveaxis(band(k), 1, 0), jnp.moveaxis(band(v), 1, 0), jnp.arange(nb)))
    ox = jnp.moveaxis(o, 0, 1).reshape(B, n, D_HEADS * D_HD)
    oc = None
    if ctx_out:
        qc = pc[0].reshape(B, m, D_KV_HEADS, G, D_HD)
        s = (jnp.einsum('bqkgd,bckd->bkgqc', qc, kc) * scale).astype(jnp.float32)
        s_sink = jnp.broadcast_to(sink_b, s.shape[:-1] + (1,))
        p = jax.nn.softmax(jnp.concatenate([s, s_sink], axis=-1), axis=-1)[..., :m].astype(vc.dtype)
        oc = jnp.einsum('bkgqc,bckd->bqkgd', p, vc).reshape(B, m, D_HEADS * D_HD)
    return ox, oc


def merge_branches(outs, gate_logits, w_branch, w_out):
    gl = gate_logits.reshape(gate_logits.shape[:-1] + (N_BRANCH, D_MODEL))
    y = None
    for j in range(N_BRANCH):
        term = jax.nn.sigmoid(gl[..., j, :]) * (outs[j] @ w_branch[j])
        y = term if y is None else y + term
    return y @ w_out


def peer_ffn(h, wq, keys, u, v):
    B, n, D = h.shape
    hd = PEER_DK // 2
    KK = PEER_TOPK * PEER_TOPK

    def block(t):
        tb = t.shape[0]
        q = (t @ wq).reshape(tb, PEER_HEADS, 2, hd)
        s = jnp.einsum('thpd,hpkd->thpk', q, keys).astype(jnp.float32)
        sv, si = lax.top_k(s, PEER_TOPK)
        cand_s = (sv[:, :, 0, :, None] + sv[:, :, 1, None, :]).reshape(tb, PEER_HEADS, KK)
        cand_i = (si[:, :, 0, :, None] * N_KEYS + si[:, :, 1, None, :]).reshape(tb, PEER_HEADS, KK)
        top_s, pos = lax.top_k(cand_s, PEER_TOPK)
        idx = jnp.take_along_axis(cand_i, pos, axis=-1)
        g = jax.nn.softmax(top_s, axis=-1)
        act = jax.nn.gelu(jnp.einsum('td,thkd->thk', t, u[idx]).astype(jnp.float32), approximate=False)
        return jnp.einsum('thk,thkd->td', (g * act).astype(t.dtype), v[idx])

    out = lax.map(block, h.reshape((B * n) // PEER_TOK_BLOCK, PEER_TOK_BLOCK, D))
    return out.reshape(B, n, D)


def setup_inputs(seed: int = 0) -> dict:
    key = jax.random.key(seed)
    ks = iter(jax.random.split(key, 40))
    nrm = lambda shape, scale=1.0: jax.random.normal(next(ks), shape, jnp.float32) * scale
    gain = lambda shape: 1.0 + nrm(shape, 0.02)
    L, D = DEPTH, D_MODEL
    return {
        'x': nrm((BATCH, SEQ, D)),
        'c': nrm((BATCH, D)),
        'ctx': nrm((BATCH, CTX_LEN, D)),
        'c_ctx': nrm((D,)),
        'w_mod': nrm((L, D, 6 * D), 0.5 * D ** -0.5),
        'b_mod': nrm((L, 6 * D), 0.01),
        'norm1_g': gain((L, D)),
        'norm2_g': gain((L, D)),
        'w_in': nrm((L, D, P_IN), D ** -0.5),
        'diff_lam_q1': nrm((L, A_QK), 0.1),
        'diff_lam_k1': nrm((L, A_QK), 0.1),
        'diff_lam_q2': nrm((L, A_QK), 0.1),
        'diff_lam_k2': nrm((L, A_QK), 0.1),
        'diff_norm_g': gain((L, A_V)),
        'mla_qnorm_g': gain((L, B_Q_LORA)),
        'mla_kvnorm_g': gain((L, B_KV_LORA)),
        'mla_w_uq': nrm((L, B_Q_LORA, B_HEADS * (B_NOPE + B_ROPE)), B_Q_LORA ** -0.5),
        'mla_w_ukv': nrm((L, B_KV_LORA, B_HEADS * (B_NOPE + B_V)), B_KV_LORA ** -0.5),
        'hgrn_lb': nrm((2, L, C_HEADS * C_DK), 0.5),
        'hgrn_norm_g': gain((L, C_DV)),
        'win_sink': nrm((L, D_HEADS), 0.5),
        'w_branch': nrm((L, N_BRANCH, BR_W, D), BR_W ** -0.5),
        'w_out': nrm((L, D, D), D ** -0.5),
        'peer_wq': nrm((L, D, PEER_HEADS * PEER_DK), D ** -0.5),
        'peer_keys': nrm((L, PEER_HEADS, 2, N_KEYS, PEER_DK // 2), (PEER_DK // 2) ** -0.5),
        'peer_u': nrm((L, N_EXPERTS, D), D ** -0.5),
        'peer_v': nrm((L, N_EXPERTS, D), 0.5),
        'final_g': gain((D,)),
    }


def reference(x, c, ctx, c_ctx, w_mod, b_mod, norm1_g, norm2_g, w_in, diff_lam_q1, diff_lam_k1,
              diff_lam_q2, diff_lam_k2, diff_norm_g, mla_qnorm_g, mla_kvnorm_g, mla_w_uq, mla_w_ukv,
              hgrn_lb, hgrn_norm_g, win_sink, w_branch, w_out, peer_wq, peer_keys, peer_u, peer_v, final_g):
    n = x.shape[1]
    ang = axial_angles(n, ROPE_DIM)
    lb_p = jax.nn.softmax(hgrn_lb.astype(jnp.float32), axis=1)
    lb = jnp.cumsum(lb_p, axis=1) - lb_p[:, :1]
    s_c = jax.nn.silu(c)
    s_cc = jax.nn.silu(c_ctx)
    xc = ctx
    for l in range(DEPTH):
        last = l == DEPTH - 1
        ctx_out = not last
        mod = s_c @ w_mod[l] + b_mod[l]
        sh1, sc1, g1, sh2, sc2, g2 = [t[:, None, :] for t in jnp.split(mod, 6, axis=-1)]
        n_cm = 2 if last else 6
        mod_c = jnp.split(s_cc @ w_mod[l][:, :n_cm * D_MODEL] + b_mod[l][:n_cm * D_MODEL], n_cm, axis=-1)

        hx = modulate(rmsnorm(x, norm1_g[l]), sh1, sc1)
        hc = modulate(rmsnorm(xc, norm1_g[l]), mod_c[0], mod_c[1])
        px = project(hx, w_in[l], None)
        pc = project(hc, w_in[l], CTX_KV_ONLY if last else None)
        oa = mixer_diff(px[0:5], pc[0:5], diff_lam_q1[l], diff_lam_k1[l], diff_lam_q2[l], diff_lam_k2[l],
                        diff_norm_g[l], l, ang, ctx_out)
        ob = mixer_mla(px[5:8], pc[5:8], mla_qnorm_g[l], mla_kvnorm_g[l], mla_w_uq[l], mla_w_ukv[l], ang, ctx_out)
        oh = mixer_hgrn(px[8:13], pc[8:13], lb[0, l], lb[1, l], hgrn_norm_g[l], ctx_out)
        od = mixer_window(px[13:16], pc[13:16], win_sink[l], ang, ctx_out)
        branches = (oa, ob, oh, od)
        x = x + g1 * merge_branches([o[0] for o in branches], px[16], w_branch[l], w_out[l])
        if ctx_out:
            xc = xc + mod_c[2] * merge_branches([o[1] for o in branches], pc[16], w_branch[l], w_out[l])

        x = x + g2 * peer_ffn(modulate(rmsnorm(x, norm2_g[l]), sh2, sc2),
                              peer_wq[l], peer_keys[l], peer_u[l], peer_v[l])
        if ctx_out:
            xc = xc + mod_c[5] * peer_ffn(modulate(rmsnorm(xc, norm2_g[l]), mod_c[3], mod_c[4]),
                                          peer_wq[l], peer_keys[l], peer_u[l], peer_v[l])
    return rmsnorm(x, final_g)
```

```python
import functools
import math

import jax
import jax.numpy as jnp
import numpy as np
from jax import lax
from jax.experimental import pallas as pl
from jax.experimental.pallas import tpu as pltpu

F32 = jnp.float32
BF16 = jnp.bfloat16

D_MODEL = 1024
GRID_W = 64
EPS = 1e-6
ROPE_BASE = 10000.0
ROPE_DIM = 64
MASK_VALUE = -1e30
A_HEADS, A_QK, A_V = 4, 64, 128
B_HEADS, B_Q_LORA, B_KV_LORA, B_NOPE, B_ROPE, B_V = 4, 256, 128, 128, 64, 128
C_HEADS, C_DK, C_DV = 4, 128, 128
D_HEADS, D_KV_HEADS, D_HD, WINDOW = 8, 2, 64, 128
N_BRANCH, BR_W = 4, 512
PEER_HEADS, N_KEYS, PEER_DK, PEER_TOPK = 8, 128, 128, 16

LANES = 128
SUBLANES = 8
VMEM_LIMIT = 56 * 1024 * 1024
SCAN_CHUNK = 64

ROPE_COLS = 1792
PLAIN_COLS = 7680


def _cparams(sem):
    return pltpu.CompilerParams(dimension_semantics=sem, vmem_limit_bytes=VMEM_LIMIT)


def _dot(a, b):
    return jnp.dot(a, b, preferred_element_type=F32)


def _dot_nt(a, b):
    return lax.dot_general(a, b, (((1,), (1,)), ((), ())), preferred_element_type=F32)


def _sigmoid(x):
    return 1.0 / (1.0 + jnp.exp(-x))


def _rms(x, g):
    return x * lax.rsqrt(jnp.mean(x * x, axis=-1, keepdims=True) + EPS) * g


def _swap16(x):
    lane = lax.broadcasted_iota(jnp.int32, x.shape, 1)
    up = pltpu.roll(x, LANES - 16, 1)
    dn = pltpu.roll(x, 16, 1)
    return jnp.where((lane & 31) < 16, up, dn)


def _rope128(x, cos, sin):
    return x * cos + _swap16(x) * sin


def _norm_mod_body(x_ref, g_ref, sh_ref, sc_ref, o_ref):
    y = _rms(x_ref[...], g_ref[...])
    o_ref[...] = (y * (1.0 + sc_ref[...]) + sh_ref[...]).astype(o_ref.dtype)


def _norm_body(x_ref, g_ref, o_ref):
    o_ref[...] = _rms(x_ref[...], g_ref[...]).astype(o_ref.dtype)


def norm_mod(x, g, shift, scale):
    B, n, D = x.shape
    tm = min(n, 512)
    row = pl.BlockSpec((None, tm, D), lambda b, i: (b, i, 0))
    vec = pl.BlockSpec((None, 1, D), lambda b, i: (b, 0, 0))
    return pl.pallas_call(
        _norm_mod_body, grid=(B, n // tm),
        in_specs=[row, pl.BlockSpec((1, D), lambda b, i: (0, 0)), vec, vec],
        out_specs=row, out_shape=jax.ShapeDtypeStruct((B, n, D), BF16),
        compiler_params=_cparams(("parallel", "parallel")), name="norm_mod",
    )(x, g.reshape(1, D), shift, scale)


def final_norm(x, g):
    B, n, D = x.shape
    tm = min(n, 512)
    row = pl.BlockSpec((None, tm, D), lambda b, i: (b, i, 0))
    return pl.pallas_call(
        _norm_body, grid=(B, n // tm),
        in_specs=[row, pl.BlockSpec((1, D), lambda b, i: (0, 0))],
        out_specs=row, out_shape=jax.ShapeDtypeStruct((B, n, D), x.dtype),
        compiler_params=_cparams(("parallel", "parallel")), name="final_norm",
    )(x, g.reshape(1, D))


def _mod_body(c_ref, w_ref, b_ref, o_ref):
    c = c_ref[...]
    s = (c * _sigmoid(c)).astype(BF16)
    o_ref[...] = _dot(s, w_ref[...]) + b_ref[...]


def mod_vectors(c16, w, b):
    R, D = c16.shape
    N = w.shape[1]
    tn = 1536
    return pl.pallas_call(
        _mod_body, grid=(N // tn,),
        in_specs=[pl.BlockSpec((R, D), lambda j: (0, 0)),
                  pl.BlockSpec((D, tn), lambda j: (0, j)),
                  pl.BlockSpec((1, tn), lambda j: (0, j))],
        out_specs=pl.BlockSpec((R, tn), lambda j: (0, j)),
        out_shape=jax.ShapeDtypeStruct((R, N), F32),
        compiler_params=_cparams(("parallel",)), name="mod_vectors",
    )(c16, w, b.reshape(1, N))


def _proj_body(a_ref, w_ref, o_ref):
    o_ref[...] = _dot(a_ref[...], w_ref[...]).astype(o_ref.dtype)


def _proj_rope_body(a_ref, w_ref, cos_ref, sin_ref, o_ref):
    acc = _dot(a_ref[...], w_ref[...])
    cos, sin = cos_ref[...], sin_ref[...]
    for j in range(acc.shape[1] // LANES):
        blk = acc[:, j * LANES:(j + 1) * LANES]
        o_ref[:, j * LANES:(j + 1) * LANES] = _rope128(blk, cos, sin).astype(o_ref.dtype)


def project(h, w, tn, rope=None):
    B, n, K = h.shape
    N = w.shape[1]
    tm = min(n, 512)
    nt = n // tm
    a = h.reshape(B * n, K)
    in_specs = [pl.BlockSpec((tm, K), lambda j, i: (i, 0)),
                pl.BlockSpec((K, tn), lambda j, i: (0, j))]
    args = [a, w]
    body = _proj_body
    if rope is not None:
        tab = pl.BlockSpec((tm, LANES), lambda j, i: (i % nt, 0))
        in_specs += [tab, tab]
        args += list(rope)
        body = _proj_rope_body
    out = pl.pallas_call(
        body, grid=(N // tn, (B * n) // tm), in_specs=in_specs,
        out_specs=pl.BlockSpec((tm, tn), lambda j, i: (i, j)),
        out_shape=jax.ShapeDtypeStruct((B * n, N), BF16),
        compiler_params=_cparams(("parallel", "parallel")), name="project",
    )(*args)
    return out.reshape(B, n, N)


def _softmax_parts(scores, extra=None):
    m = scores[0].max(axis=-1, keepdims=True)
    for s in scores[1:]:
        m = jnp.maximum(m, s.max(axis=-1, keepdims=True))
    if extra is not None:
        m = jnp.maximum(m, extra)
    es = [jnp.exp(s - m) for s in scores]
    l = es[0].sum(axis=-1, keepdims=True)
    for e in es[1:]:
        l = l + e.sum(axis=-1, keepdims=True)
    if extra is not None:
        l = l + jnp.exp(extra - m)
    return es, 1.0 / l


def _half_mask(shape, half):
    lane = lax.broadcasted_iota(jnp.int32, shape, 1)
    return (lane < 64) if half == 0 else (lane >= 64)


def _diff_attn_body(lam_ref, *refs, with_x, post_scale):
    if with_x:
        q1_ref, q2_ref, k1x_ref, k2x_ref, vx_ref, k1c_ref, k2c_ref, vc_ref, g_ref, o_ref = refs
    else:
        q1_ref, q2_ref, k1c_ref, k2c_ref, vc_ref, g_ref, o_ref = refs
    lam = lam_ref[0]
    g = g_ref[...]
    scale = A_QK ** -0.5
    for h in range(A_HEADS):
        pair = slice((h // 2) * LANES, (h // 2 + 1) * LANES)
        vs = slice(h * A_V, (h + 1) * A_V)

        def probs(q_ref, kx_ref, kc_ref):
            q = q_ref[:, pair]
            q = jnp.where(_half_mask(q.shape, h % 2), q, jnp.zeros_like(q)) * scale
            scores = [_dot_nt(q, kx_ref[:, pair])] if with_x else []
            scores.append(_dot_nt(q, kc_ref[:, pair]))
            es, rl = _softmax_parts(scores)
            return es, rl

        if with_x:
            e1, rl1 = probs(q1_ref, k1x_ref, k1c_ref)
            e2, rl2 = probs(q2_ref, k2x_ref, k2c_ref)
        else:
            e1, rl1 = probs(q1_ref, None, k1c_ref)
            e2, rl2 = probs(q2_ref, None, k2c_ref)
        rl2 = rl2 * lam
        vals = [vx_ref[:, vs], vc_ref[:, vs]] if with_x else [vc_ref[:, vs]]
        o = None
        for a, b, v in zip(e1, e2, vals):
            p = (a * rl1 - b * rl2).astype(BF16)
            t = _dot(p, v)
            o = t if o is None else o + t
        o_ref[:, vs] = (_rms(o, g) * post_scale).astype(o_ref.dtype)


def diff_attention(lam, pr_q, pr_x, pp_x, pr_c, pp_c, norm_g, post_scale, with_x):
    B, nq, _ = pr_q.shape
    m = pr_c.shape[1]
    tq = min(nq, 256)
    qspec = lambda blk: pl.BlockSpec((None, tq, 256), lambda b, i: (b, i, blk))
    in_specs = [pl.BlockSpec(memory_space=pltpu.SMEM), qspec(0), qspec(1)]
    args = [lam, pr_q, pr_q]
    if with_x:
        n = pr_x.shape[1]
        in_specs += [pl.BlockSpec((None, n, 256), lambda b, i: (b, 0, 2)),
                     pl.BlockSpec((None, n, 256), lambda b, i: (b, 0, 3)),
                     pl.BlockSpec((None, n, 512), lambda b, i: (b, 0, 0))]
        args += [pr_x, pr_x, pp_x]
    in_specs += [pl.BlockSpec((None, m, 256), lambda b, i: (b, 0, 2)),
                 pl.BlockSpec((None, m, 256), lambda b, i: (b, 0, 3)),
                 pl.BlockSpec((None, m, 512), lambda b, i: (b, 0, 0)),
                 pl.BlockSpec((1, A_V), lambda b, i: (0, 0))]
    args += [pr_c, pr_c, pp_c, norm_g.reshape(1, A_V)]
    return pl.pallas_call(
        functools.partial(_diff_attn_body, with_x=with_x, post_scale=post_scale),
        grid=(B, nq // tq), in_specs=in_specs,
        out_specs=pl.BlockSpec((None, tq, 512), lambda b, i: (b, i, 0)),
        out_shape=jax.ShapeDtypeStruct((B, nq, 512), BF16),
        compiler_params=_cparams(("parallel", "parallel")), name="diff_attention",
    )(*args)


def _mla_up_body(cq_ref, ckv_ref, qg_ref, kvg_ref, wq_ref, wkv_ref, *refs, rope):
    if rope:
        cos_ref, sin_ref, q_ref, kv_ref = refs
    else:
        q_ref, kv_ref = refs
    cq = _rms(cq_ref[...].astype(F32), qg_ref[...]).astype(BF16)
    ckv = _rms(ckv_ref[...].astype(F32), kvg_ref[...]).astype(BF16)
    kv_ref[...] = _dot(ckv, wkv_ref[...]).astype(kv_ref.dtype)
    q = _dot(cq, wq_ref[...])
    for h in range(B_HEADS):
        nope = slice(h * 256, h * 256 + LANES)
        ropes = slice(h * 256 + LANES, (h + 1) * 256)
        q_ref[:, nope] = q[:, nope].astype(q_ref.dtype)
        blk = q[:, ropes]
        if rope:
            blk = _rope128(blk, cos_ref[...], sin_ref[...])
        q_ref[:, ropes] = blk.astype(q_ref.dtype)


def mla_up(pp, qn_g, kvn_g, w_uq, w_ukv, rope):
    B, n, _ = pp.shape
    tm = min(n, 512)
    in_specs = [pl.BlockSpec((None, tm, 256), lambda b, i: (b, i, 28)),
                pl.BlockSpec((None, tm, 128), lambda b, i: (b, i, 58)),
                pl.BlockSpec((1, 256), lambda b, i: (0, 0)),
                pl.BlockSpec((1, 128), lambda b, i: (0, 0)),
                pl.BlockSpec((256, 1024), lambda b, i: (0, 0)),
                pl.BlockSpec((128, 1024), lambda b, i: (0, 0))]
    args = [pp, pp, qn_g.reshape(1, 256), kvn_g.reshape(1, 128), w_uq, w_ukv]
    if rope is not None:
        tab = pl.BlockSpec((tm, LANES), lambda b, i: (i, 0))
        in_specs += [tab, tab]
        args += list(rope)
    out = pl.BlockSpec((None, tm, 1024), lambda b, i: (b, i, 0))
    shp = jax.ShapeDtypeStruct((B, n, 1024), BF16)
    return pl.pallas_call(
        functools.partial(_mla_up_body, rope=rope is not None), grid=(B, n // tm),
        in_specs=in_specs, out_specs=[out, out], out_shape=[shp, shp],
        compiler_params=_cparams(("parallel", "parallel")), name="mla_up",
    )(*args)


def _mla_attn_body(*refs, with_x):
    if with_x:
        q_ref, knx_ref, vx_ref, krx_ref, knc_ref, vc_ref, krc_ref, o_ref = refs
    else:
        q_ref, knc_ref, vc_ref, krc_ref, o_ref = refs
    scale = (B_NOPE + B_ROPE) ** -0.5
    for h in range(B_HEADS):
        hs = slice(h * LANES, (h + 1) * LANES)
        qn = q_ref[:, h * 256:h * 256 + LANES]
        qr = q_ref[:, h * 256 + LANES:(h + 1) * 256]
        scores = []
        if with_x:
            scores.append((_dot_nt(qn, knx_ref[:, hs]) + _dot_nt(qr, krx_ref[...])) * scale)
        scores.append((_dot_nt(qn, knc_ref[:, hs]) + _dot_nt(qr, krc_ref[...])) * scale)
        es, rl = _softmax_parts(scores)
        vals = [vx_ref[:, hs], vc_ref[:, hs]] if with_x else [vc_ref[:, hs]]
        o = None
        for e, v in zip(es, vals):
            t = _dot((e * rl).astype(BF16), v)
            o = t if o is None else o + t
        o_ref[:, hs] = o.astype(o_ref.dtype)


def mla_attention(qb, kvb_x, pr_x, kvb_c, pr_c, with_x):
    B, nq, _ = qb.shape
    m = kvb_c.shape[1]
    tq = min(nq, 256)
    in_specs = [pl.BlockSpec((None, tq, 1024), lambda b, i: (b, i, 0))]
    args = [qb]
    if with_x:
        n = kvb_x.shape[1]
        in_specs += [pl.BlockSpec((None, n, 512), lambda b, i: (b, 0, 0)),
                     pl.BlockSpec((None, n, 512), lambda b, i: (b, 0, 1)),
                     pl.BlockSpec((None, n, 128), lambda b, i: (b, 0, 13))]
        args += [kvb_x, kvb_x, pr_x]
    in_specs += [pl.BlockSpec((None, m, 512), lambda b, i: (b, 0, 0)),
                 pl.BlockSpec((None, m, 512), lambda b, i: (b, 0, 1)),
                 pl.BlockSpec((None, m, 128), lambda b, i: (b, 0, 13))]
    args += [kvb_c, kvb_c, pr_c]
    return pl.pallas_call(
        functools.partial(_mla_attn_body, with_x=with_x), grid=(B, nq // tq),
        in_specs=in_specs, out_specs=pl.BlockSpec((None, tq, 512), lambda b, i: (b, i, 0)),
        out_shape=jax.ShapeDtypeStruct((B, nq, 512), BF16),
        compiler_params=_cparams(("parallel", "parallel")), name="mla_attention",
    )(*args)


def _cumsum_rows(tri, g):
    g1 = g.astype(BF16)
    r1 = g - g1.astype(F32)
    g2 = r1.astype(BF16)
    g3 = (r1 - g2.astype(F32)).astype(BF16)
    return _dot(tri, g1) + _dot(tri, g2) + _dot(tri, g3)


def _hgrn_body(qf_ref, zf_ref, if_ref, qb_ref, zb_ref, ib_ref, lb_ref, s0_ref, of_ref, ob_ref, s_ref):
    c = pl.program_id(1)
    L = qf_ref.shape[0]

    @pl.when(c == 0)
    def _():
        s_ref[...] = s0_ref[...]

    row = lax.broadcasted_iota(jnp.int32, (L, L), 0)
    col = lax.broadcasted_iota(jnp.int32, (L, L), 1)
    for d, (q_ref, z_ref, i_ref, o_ref) in enumerate(((qf_ref, zf_ref, if_ref, of_ref),
                                                      (qb_ref, zb_ref, ib_ref, ob_ref))):
        keep = (col <= row) if d == 0 else (col >= row)
        tri = jnp.where(keep, 1.0, 0.0).astype(BF16)
        last = L - 1 if d == 0 else 0
        for h in range(C_HEADS):
            hs = slice(h * C_DK, (h + 1) * C_DK)
            lbh = lb_ref[d:d + 1, hs]
            f = lbh + (1.0 - lbh) * _sigmoid(z_ref[:, hs].astype(F32))
            key = 1.0 - f
            b = _cumsum_rows(tri, jnp.log(f))
            ref = b[L // 2:L // 2 + 1, :]
            bend = b[last:last + 1, :]
            q = q_ref[:, hs].astype(F32)
            v = i_ref[:, hs]
            att = _dot_nt((q * jnp.exp(b - ref)).astype(BF16), (key * jnp.exp(ref - b)).astype(BF16))
            att = jnp.where(keep, att, 0.0).astype(BF16)
            st = s_ref[d * C_HEADS + h]
            o = _dot_nt((q * jnp.exp(b)).astype(BF16), st.astype(BF16)) + _dot(att, v)
            o_ref[:, hs] = o
            kdec = (key * jnp.exp(bend - b)).astype(BF16)
            vt = v.astype(F32).T.astype(BF16)
            s_ref[d * C_HEADS + h] = jnp.exp(bend) * st + _dot(vt, kdec)


def hgrn_scan(pp, lb2, s0):
    B, n, _ = pp.shape
    L = SCAN_CHUNK
    nc = n // L
    fwd = lambda blk: pl.BlockSpec((None, L, 512), lambda b, c: (b, c, blk))
    bwd = lambda blk: pl.BlockSpec((None, L, 512), lambda b, c: (b, nc - 1 - c, blk))
    st = pl.BlockSpec((None, 2 * C_HEADS, C_DV, C_DK), lambda b, c: (b, 0, 0, 0))
    oshape = jax.ShapeDtypeStruct((B, n, 512), F32)
    return pl.pallas_call(
        _hgrn_body, grid=(B, nc),
        in_specs=[fwd(1), fwd(2), fwd(4), bwd(1), bwd(3), bwd(4),
                  pl.BlockSpec((2, 512), lambda b, c: (0, 0)), st],
        out_specs=[fwd(0), bwd(0), st],
        out_shape=[oshape, oshape, jax.ShapeDtypeStruct(s0.shape, F32)],
        compiler_params=_cparams(("parallel", "arbitrary")), name="hgrn_scan",
    )(pp, pp, pp, pp, pp, pp, lb2, s0)


def _place_half(x, src_half, dst_half):
    return x if src_half == dst_half else pltpu.roll(x, 64, 1)


def _window_body(sink_ref, q_ref, *refs, local):
    if local:
        kp_ref, ko_ref, kn_ref, vp_ref, vo_ref, vn_ref, kc_ref, vc_ref, o_ref = refs
    else:
        kc_ref, vc_ref, o_ref = refs
    i = pl.program_id(1)
    nb = pl.num_programs(1)
    tq = q_ref.shape[0]
    G = D_HEADS // D_KV_HEADS
    if local:
        W = WINDOW
        k_loc = jnp.concatenate([kp_ref[...], ko_ref[...], kn_ref[...]], axis=0)
        v_loc = jnp.concatenate([vp_ref[...], vo_ref[...], vn_ref[...]], axis=0)
        a = lax.broadcasted_iota(jnp.int32, (W, 3 * W), 0)
        j = lax.broadcasted_iota(jnp.int32, (W, 3 * W), 1)
        rel = j - W - a
        kblk = i + (j >> 7) - 1
        valid = (jnp.abs(rel) <= W) & (kblk >= 0) & (kblk < nb)
    kc, vc = kc_ref[...], vc_ref[...]
    lo = _half_mask((tq, LANES), 0)
    for pair in range(D_HEADS // 2):
        kh = (2 * pair) // G
        qp = q_ref[:, pair * LANES:(pair + 1) * LANES].astype(F32)
        outs = []
        for sub in range(2):
            hq = 2 * pair + sub
            q = _place_half(qp, sub, kh)
            q = (jnp.where(_half_mask(q.shape, kh), q, 0.0) * (D_HD ** -0.5)).astype(BF16)
            scores = []
            if local:
                scores.append(jnp.where(valid, _dot_nt(q, k_loc), MASK_VALUE))
            scores.append(_dot_nt(q, kc))
            es, rl = _softmax_parts(scores, extra=sink_ref[hq])
            vals = [v_loc, vc] if local else [vc]
            r = None
            for e, v in zip(es, vals):
                t = _dot((e * rl).astype(BF16), v)
                r = t if r is None else r + t
            outs.append(_place_half(r, kh, sub))
        o_ref[:, pair * LANES:(pair + 1) * LANES] = jnp.where(lo, outs[0], outs[1]).astype(o_ref.dtype)


def window_attention(sink, pr_q, pp_x, pr_c, pp_c, local):
    B, nq, _ = pr_q.shape
    m = pr_c.shape[1]
    tq = WINDOW if local else min(nq, 256)
    nb = nq // tq
    in_specs = [pl.BlockSpec(memory_space=pltpu.SMEM),
                pl.BlockSpec((None, tq, 512), lambda b, i: (b, i, 2))]
    args = [sink, pr_q]
    if local:
        prev = lambda b, i: jnp.maximum(i - 1, 0)
        nxt = lambda b, i: jnp.minimum(i + 1, nb - 1)
        for blk, arr in ((12, pr_q), (59, pp_x)):
            in_specs += [pl.BlockSpec((None, tq, 128), lambda b, i, blk=blk: (b, prev(b, i), blk)),
                         pl.BlockSpec((None, tq, 128), lambda b, i, blk=blk: (b, i, blk)),
                         pl.BlockSpec((None, tq, 128), lambda b, i, blk=blk: (b, nxt(b, i), blk))]
            args += [arr, arr, arr]
    in_specs += [pl.BlockSpec((None, m, 128), lambda b, i: (b, 0, 12)),
                 pl.BlockSpec((None, m, 128), lambda b, i: (b, 0, 59))]
    args += [pr_c, pp_c]
    return pl.pallas_call(
        functools.partial(_window_body, local=local), grid=(B, nb), in_specs=in_specs,
        out_specs=pl.BlockSpec((None, tq, 512), lambda b, i: (b, i, 0)),
        out_shape=jax.ShapeDtypeStruct((B, nq, 512), BF16),
        compiler_params=_cparams(("parallel", "parallel")), name="window_attention",
    )(*args)


def _merge_body(oa_ref, ob_ref, of_ref, obk_ref, hg_ref, hng_ref, od_ref, g0_ref, g1_ref, g2_ref, g3_ref,
                wb_ref, wo_ref, x_ref, gate_ref, o_ref):
    hng = hng_ref[...]
    oc = of_ref[...] + obk_ref[...]
    parts = []
    for h in range(C_HEADS):
        hs = slice(h * C_DV, (h + 1) * C_DV)
        g = hg_ref[:, hs].astype(F32)
        parts.append((_rms(oc[:, hs], hng) * (g * _sigmoid(g))).astype(BF16))
    oh = jnp.concatenate(parts, axis=1)
    y = None
    branches = (oa_ref[...], ob_ref[...], oh, od_ref[...])
    for j, (br, gl_ref) in enumerate(zip(branches, (g0_ref, g1_ref, g2_ref, g3_ref))):
        term = _sigmoid(gl_ref[...].astype(F32)) * _dot(br, wb_ref[j])
        y = term if y is None else y + term
    o_ref[...] = x_ref[...] + gate_ref[...] * _dot(y.astype(BF16), wo_ref[...])


def merge(oa, ob, of, obk, pp, hgrn_g, od, w_branch, w_out, x, gate):
    B, n, D = x.shape
    tm = min(n, 256)
    blk = lambda w, k: pl.BlockSpec((None, tm, w), lambda b, i: (b, i, k))
    in_specs = [blk(512, 0), blk(512, 0), blk(512, 0), blk(512, 0), blk(512, 5),
                pl.BlockSpec((1, C_DV), lambda b, i: (0, 0)), blk(512, 0),
                blk(1024, 3), blk(1024, 4), blk(1024, 5), blk(1024, 6),
                pl.BlockSpec((N_BRANCH, BR_W, D), lambda b, i: (0, 0, 0)),
                pl.BlockSpec((D, D), lambda b, i: (0, 0)),
                blk(D, 0), pl.BlockSpec((None, 1, D), lambda b, i: (b, 0, 0))]
    return pl.pallas_call(
        _merge_body, grid=(B, n // tm), in_specs=in_specs, out_specs=blk(D, 0),
        out_shape=jax.ShapeDtypeStruct((B, n, D), F32),
        compiler_params=_cparams(("parallel", "parallel")), name="merge",
    )(oa, ob, of, obk, pp, hgrn_g.reshape(1, C_DV), od, pp, pp, pp, pp, w_branch, w_out, x, gate)


N_SORTED = PEER_TOPK + 1
PAIR_CANDS = [(a, b) for a in range(N_SORTED) for b in range(N_SORTED) if (a + 1) * (b + 1) <= N_SORTED]


def _peer_route_body(x_ref, g_ref, sh_ref, sc_ref, wqt_ref, kcat_ref,
                     h_ref, na_ref, e1_ref, s2_ref, e2_ref, work_ref, sv_ref):
    tb = x_ref.shape[0]
    H, NK = PEER_HEADS, N_KEYS
    y = _rms(x_ref[...], g_ref[...])
    hmod = (y * (1.0 + sc_ref[...]) + sh_ref[...]).astype(BF16)
    h_ref[...] = hmod
    qt = _dot_nt(wqt_ref[...], hmod).astype(BF16)
    s_all = _dot(kcat_ref[...], qt)
    s1 = s_all[0:NK * H].reshape(NK, H, tb)
    s2 = s_all[NK * H:2 * NK * H].reshape(NK, H, tb)
    s2h = s_all[2 * NK * H:3 * NK * H]

    for p, s in enumerate((s1, s2)):
        work_ref[...] = s

        def body(r, carry, p=p):
            w = work_ref[...]
            mx = jnp.max(w, axis=0)
            sv_ref[p, r] = mx
            work_ref[...] = jnp.where(w == mx[None], -jnp.inf, w)
            return carry

        lax.fori_loop(0, N_SORTED, body, 0)

    sv1 = [sv_ref[0, a] for a in range(N_SORTED)]
    sv2 = [sv_ref[1, b] for b in range(N_SORTED)]
    cands = [sv1[a] + sv2[b] for a, b in PAIR_CANDS]
    cur = list(cands)
    t16 = t17 = None
    for r in range(N_SORTED):
        mx = functools.reduce(jnp.maximum, cur)
        if r == PEER_TOPK - 1:
            t16 = mx
        if r == PEER_TOPK:
            t17 = mx
        cur = [jnp.where(cc == mx, -jnp.inf, cc) for cc in cur]
    tau = 0.5 * (t16 + t17)
    top = sv1[0] + sv2[0]
    z = functools.reduce(lambda u, w: u + w, [jnp.where(cc >= tau, jnp.exp(cc - top), 0.0) for cc in cands])
    rz = 1.0 / z
    na_ref[...] = (tau[None] - s1).reshape(NK * H, tb)
    e1_ref[...] = (jnp.exp(s1 - sv1[0][None]) * rz[None]).reshape(NK * H, tb)
    s2_ref[...] = s2h
    m2 = sv2[0]
    for h in range(H):
        rows = slice(h * NK, (h + 1) * NK)
        e2_ref[rows, :] = jnp.exp(s2h[rows, :] - m2[h:h + 1, :])


def peer_route(x, g, shift, scale, wqt, kcat):
    B, n, D = x.shape
    tb = min(n, 256)
    nt = n // tb
    N = B * n
    R = N_KEYS * PEER_HEADS
    xf = x.reshape(N, D)
    vec = pl.BlockSpec((None, 1, D), lambda t: (t // nt, 0, 0))
    col = pl.BlockSpec((R, tb), lambda t: (0, t))
    cshape = jax.ShapeDtypeStruct((R, N), F32)
    return pl.pallas_call(
        _peer_route_body, grid=(N // tb,),
        in_specs=[pl.BlockSpec((tb, D), lambda t: (t, 0)), pl.BlockSpec((1, D), lambda t: (0, 0)), vec, vec,
                  pl.BlockSpec((D, D), lambda t: (0, 0)), pl.BlockSpec((3 * R, D), lambda t: (0, 0))],
        out_specs=[pl.BlockSpec((tb, D), lambda t: (t, 0)), col, col, col, col],
        out_shape=[jax.ShapeDtypeStruct((N, D), BF16), cshape, cshape, cshape, cshape],
        scratch_shapes=[pltpu.VMEM((N_KEYS, PEER_HEADS, tb), F32),
                        pltpu.VMEM((2, N_SORTED, PEER_HEADS, tb), F32)],
        compiler_params=_cparams(("parallel",)), name="peer_route",
    )(xf, g.reshape(1, D), shift, scale, wqt, kcat)


def _gelu(x):
    return 0.5 * x * (1.0 + lax.erf(x * math.sqrt(0.5)))


def _peer_dense_body(h_ref, u_ref, vt_ref, na_ref, e1_ref, s2_ref, e2_ref, x_ref, gate_ref, o_ref,
                     acc_ref, p_ref, *, ib_per_step):
    e = pl.program_id(1)
    H, NK = PEER_HEADS, N_KEYS

    @pl.when(e == 0)
    def _():
        acc_ref[...] = jnp.zeros_like(acc_ref)

    act_all = _dot_nt(u_ref[...], h_ref[...])
    for ib in range(ib_per_step):
        rows = slice(ib * NK, (ib + 1) * NK)
        act = _gelu(act_all[rows, :])
        r0 = pl.multiple_of((e * ib_per_step + ib) * H, SUBLANES)
        na8 = na_ref[pl.ds(r0, H), :]
        e18 = e1_ref[pl.ds(r0, H), :]
        w = None
        for h in range(H):
            hr = slice(h * NK, (h + 1) * NK)
            term = jnp.where(s2_ref[hr, :] >= na8[h:h + 1, :], e2_ref[hr, :], 0.0) * e18[h:h + 1, :]
            w = term if w is None else w + term
        p_ref[rows, :] = (act * w).astype(BF16)
    acc_ref[...] += _dot(vt_ref[...], p_ref[...])

    @pl.when(e == pl.num_programs(1) - 1)
    def _():
        o_ref[...] = x_ref[...] + gate_ref[...] * acc_ref[...].T


def peer_dense(hmod, u, vt, na, e1, s2, e2, x, gate):
    B, n, D = x.shape
    N = B * n
    tb = min(n, 512)
    nt = n // tb
    ib = 8
    ec = ib * N_KEYS
    n_exp = u.shape[0]
    R = N_KEYS * PEER_HEADS
    col = pl.BlockSpec((R, tb), lambda t, e: (0, t))
    out = pl.pallas_call(
        functools.partial(_peer_dense_body, ib_per_step=ib), grid=(N // tb, n_exp // ec),
        in_specs=[pl.BlockSpec((tb, D), lambda t, e: (t, 0)),
                  pl.BlockSpec((ec, D), lambda t, e: (e, 0)),
                  pl.BlockSpec((D, ec), lambda t, e: (0, e)),
                  col, col, col, col,
                  pl.BlockSpec((tb, D), lambda t, e: (t, 0)),
                  pl.BlockSpec((None, 1, D), lambda t, e: (t // nt, 0, 0))],
        out_specs=pl.BlockSpec((tb, D), lambda t, e: (t, 0)),
        out_shape=jax.ShapeDtypeStruct((N, D), F32),
        scratch_shapes=[pltpu.VMEM((D, tb), F32), pltpu.VMEM((ec, tb), BF16)],
        compiler_params=_cparams(("parallel", "arbitrary")), name="peer_dense",
    )(hmod, u, vt, na, e1, s2, e2, x.reshape(N, D), gate)
    return out.reshape(B, n, D)


def _rope_tables(n):
    rows = n // GRID_W
    r = jnp.repeat(jnp.arange(rows, dtype=F32), GRID_W)
    col = jnp.tile(jnp.arange(GRID_W, dtype=F32), rows)
    m = ROPE_DIM // 2
    freqs = ROPE_BASE ** (-2.0 * jnp.arange(m // 2, dtype=F32) / m)
    ar, ac = r[:, None] * freqs, col[:, None] * freqs
    cos = jnp.concatenate([jnp.cos(ar), jnp.cos(ar), jnp.cos(ac), jnp.cos(ac)], axis=-1)
    sin = jnp.concatenate([-jnp.sin(ar), jnp.sin(ar), -jnp.sin(ac), jnp.sin(ac)], axis=-1)
    return jnp.tile(cos, (1, 2)), jnp.tile(sin, (1, 2))


def _prep_w_in(w):
    sizes = (256, 256, 256, 256, 512, 256, 128, 64, 512, 512, 512, 512, 512, 512, 128, 128, 4096)
    offs = np.concatenate([[0], np.cumsum(sizes)])
    piece = lambda k: w[:, offs[k]:offs[k + 1]]
    zeros = jnp.zeros((w.shape[0], 64), w.dtype)
    w_rope = jnp.concatenate([piece(0), piece(1), piece(2), piece(3), piece(13), piece(14), piece(7), zeros], axis=1)
    w_plain = jnp.concatenate([piece(4), piece(8), piece(9), piece(10), piece(11), piece(12), piece(16),
                               piece(5), piece(6), piece(15)], axis=1)
    return w_rope.astype(BF16), w_plain.astype(BF16)


def _prep_mla(w_uq, w_ukv):
    wq = w_uq.reshape(B_Q_LORA, B_HEADS, B_NOPE + B_ROPE)
    wq = jnp.pad(wq, ((0, 0), (0, 0), (0, 256 - B_NOPE - B_ROPE))).reshape(B_Q_LORA, B_HEADS * 256)
    wkv = w_ukv.reshape(B_KV_LORA, B_HEADS, B_NOPE + B_V)
    wkv = jnp.concatenate([wkv[:, :, :B_NOPE].reshape(B_KV_LORA, -1), wkv[:, :, B_NOPE:].reshape(B_KV_LORA, -1)], axis=1)
    return wq.astype(BF16), wkv.astype(BF16)


def _prep_peer_keys(keys):
    H, NK, hd = PEER_HEADS, N_KEYS, PEER_DK // 2
    eye = jnp.eye(H, dtype=keys.dtype)

    def place(p, head_major):
        k = keys[:, p]
        sel = jnp.zeros((2,), keys.dtype).at[p].set(1.0)
        full = jnp.einsum('hkd,hg,p->khgpd', k, eye, sel)
        if head_major:
            full = full.transpose(1, 0, 2, 3, 4)
        return full.reshape(NK * H, H * 2 * hd)

    return jnp.concatenate([place(0, False), place(1, False), place(1, True)], axis=0).astype(BF16)


def kernel(x, c, ctx, c_ctx, w_mod, b_mod, norm1_g, norm2_g, w_in, diff_lam_q1, diff_lam_k1, diff_lam_q2,
           diff_lam_k2, diff_norm_g, mla_qnorm_g, mla_kvnorm_g, mla_w_uq, mla_w_ukv, hgrn_lb, hgrn_norm_g,
           win_sink, w_branch, w_out, peer_wq, peer_keys, peer_u, peer_v, final_g):
    B, n, D = x.shape
    depth = w_mod.shape[0]
    rope = _rope_tables(n)
    lb_p = jax.nn.softmax(hgrn_lb.astype(F32), axis=1)
    lb = jnp.cumsum(lb_p, axis=1) - lb_p[:, :1]
    c16 = jnp.concatenate([c, c_ctx[None], jnp.zeros((16 - B - 1, D), F32)], axis=0)
    xc = ctx
    for l in range(depth):
        last = l == depth - 1
        mod = mod_vectors(c16, w_mod[l].astype(BF16), b_mod[l])
        mx = [mod[:B, k * D:(k + 1) * D][:, None, :] for k in range(6)]
        mc = [jnp.broadcast_to(mod[B, k * D:(k + 1) * D][None, None, :], (B, 1, D)) for k in range(6)]
        w_rope, w_plain = _prep_w_in(w_in[l])
        w_uq, w_ukv = _prep_mla(mla_w_uq[l], mla_w_ukv[l])
        wb, wo = w_branch[l].astype(BF16), w_out[l].astype(BF16)

        hx = norm_mod(x, norm1_g[l], mx[0], mx[1])
        hc = norm_mod(xc, norm1_g[l], mc[0], mc[1])
        pr_x = project(hx, w_rope, 896, rope)
        pp_x = project(hx, w_plain, 1536)
        pr_c = project(hc, w_rope, 896)
        pp_c = project(hc, w_plain, 1536)

        lam_init = 0.8 - 0.6 * math.exp(-0.3 * l)
        lam = (jnp.exp(jnp.sum(diff_lam_q1[l] * diff_lam_k1[l])) - jnp.exp(jnp.sum(diff_lam_q2[l] * diff_lam_k2[l]))
               + lam_init).reshape(1).astype(F32)
        qb_x, kvb_x = mla_up(pp_x, mla_qnorm_g[l], mla_kvnorm_g[l], w_uq, w_ukv, rope)
        qb_c, kvb_c = mla_up(pp_c, mla_qnorm_g[l], mla_kvnorm_g[l], w_uq, w_ukv, None)
        lb2 = lb[:, l, :]
        s0 = jnp.zeros((B, 2 * C_HEADS, C_DV, C_DK), F32)
        of_c, ob_c, s_ctx = hgrn_scan(pp_c, lb2, s0)
        sink = win_sink[l].astype(F32)

        oa = diff_attention(lam, pr_x, pr_x, pp_x, pr_c, pp_c, diff_norm_g[l], 1.0 - lam_init, True)
        ob = mla_attention(qb_x, kvb_x, pr_x, kvb_c, pr_c, True)
        of_x, ob_x, _ = hgrn_scan(pp_x, lb2, s_ctx)
        od = window_attention(sink, pr_x, pp_x, pr_c, pp_c, True)
        x_new = merge(oa, ob, of_x, ob_x, pp_x, hgrn_norm_g[l], od, wb, wo, x, mx[2])
        if not last:
            oa_c = diff_attention(lam, pr_c, None, None, pr_c, pp_c, diff_norm_g[l], 1.0 - lam_init, False)
            ob_c2 = mla_attention(qb_c, None, None, kvb_c, pr_c, False)
            od_c = window_attention(sink, pr_c, None, pr_c, pp_c, False)
            xc = merge(oa_c, ob_c2, of_c, ob_c, pp_c, hgrn_norm_g[l], od_c, wb, wo, xc, mc[2])
        x = x_new

        wqt = peer_wq[l].T.astype(BF16)
        kcat = _prep_peer_keys(peer_keys[l])
        u = peer_u[l].astype(BF16)
        vt = peer_v[l].T.astype(BF16)
        hmod, na, e1, s2, e2 = peer_route(x, norm2_g[l], mx[3], mx[4], wqt, kcat)
        x = peer_dense(hmod, u, vt, na, e1, s2, e2, x, mx[5])
        if not last:
            hmod, na, e1, s2, e2 = peer_route(xc, norm2_g[l], mc[3], mc[4], wqt, kcat)
            xc = peer_dense(hmod, u, vt, na, e1, s2, e2, xc, mc[5])
    return final_norm(x, final_g)
```

```python
import functools
import math

import jax
import jax.numpy as jnp
import numpy as np
from jax import lax
from jax.experimental import pallas as pl
from jax.experimental.pallas import tpu as pltpu

F32 = jnp.float32
BF16 = jnp.bfloat16

D_MODEL = 1024
GRID_W = 64
EPS = 1e-6
ROPE_BASE = 10000.0
ROPE_DIM = 64
MASK_VALUE = -1e30
A_HEADS, A_QK, A_V = 4, 64, 128
B_HEADS, B_Q_LORA, B_KV_LORA, B_NOPE, B_ROPE, B_V = 4, 256, 128, 128, 64, 128
C_HEADS, C_DK, C_DV = 4, 128, 128
D_HEADS, D_KV_HEADS, D_HD, WINDOW = 8, 2, 64, 128
N_BRANCH, BR_W = 4, 512
PEER_HEADS, N_KEYS, PEER_DK, PEER_TOPK = 8, 128, 128, 16

LANES = 128
SUBLANES = 8
VMEM_LIMIT = 56 * 1024 * 1024
SCAN_CHUNK = 64

ROPE_COLS = 1792
PLAIN_COLS = 7680


def _cparams(sem):
    return pltpu.CompilerParams(dimension_semantics=sem, vmem_limit_bytes=VMEM_LIMIT)


def _dot(a, b):
    return jnp.dot(a, b, preferred_element_type=F32)


def _dot_nt(a, b):
    return lax.dot_general(a, b, (((1,), (1,)), ((), ())), preferred_element_type=F32)


def _sigmoid(x):
    return 1.0 / (1.0 + jnp.exp(-x))


def _rms(x, g):
    return x * lax.rsqrt(jnp.mean(x * x, axis=-1, keepdims=True) + EPS) * g


def _swap16(x):
    lane = lax.broadcasted_iota(jnp.int32, x.shape, 1)
    up = pltpu.roll(x, LANES - 16, 1)
    dn = pltpu.roll(x, 16, 1)
    return jnp.where((lane & 31) < 16, up, dn)


def _rope128(x, cos, sin):
    return x * cos + _swap16(x) * sin


def _norm_mod_body(x_ref, g_ref, sh_ref, sc_ref, o_ref):
    y = _rms(x_ref[...], g_ref[...])
    o_ref[...] = (y * (1.0 + sc_ref[...]) + sh_ref[...]).astype(o_ref.dtype)


def _norm_body(x_ref, g_ref, o_ref):
    o_ref[...] = _rms(x_ref[...], g_ref[...]).astype(o_ref.dtype)


def norm_mod(x, g, shift, scale):
    B, n, D = x.shape
    tm = min(n, 512)
    row = pl.BlockSpec((None, tm, D), lambda b, i: (b, i, 0))
    vec = pl.BlockSpec((None, 1, D), lambda b, i: (b, 0, 0))
    return pl.pallas_call(
        _norm_mod_body, grid=(B, n // tm),
        in_specs=[row, pl.BlockSpec((1, D), lambda b, i: (0, 0)), vec, vec],
        out_specs=row, out_shape=jax.ShapeDtypeStruct((B, n, D), BF16),
        compiler_params=_cparams(("parallel", "parallel")), name="norm_mod",
    )(x, g.reshape(1, D), shift, scale)


def final_norm(x, g):
    B, n, D = x.shape
    tm = min(n, 512)
    row = pl.BlockSpec((None, tm, D), lambda b, i: (b, i, 0))
    return pl.pallas_call(
        _norm_body, grid=(B, n // tm),
        in_specs=[row, pl.BlockSpec((1, D), lambda b, i: (0, 0))],
        out_specs=row, out_shape=jax.ShapeDtypeStruct((B, n, D), x.dtype),
        compiler_params=_cparams(("parallel", "parallel")), name="final_norm",
    )(x, g.reshape(1, D))


def _mod_body(c_ref, w_ref, b_ref, o_ref):
    c = c_ref[...]
    s = (c * _sigmoid(c)).astype(BF16)
    o_ref[...] = _dot(s, w_ref[...]) + b_ref[...]


def mod_vectors(c16, w, b):
    R, D = c16.shape
    N = w.shape[1]
    tn = 1536
    return pl.pallas_call(
        _mod_body, grid=(N // tn,),
        in_specs=[pl.BlockSpec((R, D), lambda j: (0, 0)),
                  pl.BlockSpec((D, tn), lambda j: (0, j)),
                  pl.BlockSpec((1, tn), lambda j: (0, j))],
        out_specs=pl.BlockSpec((R, tn), lambda j: (0, j)),
        out_shape=jax.ShapeDtypeStruct((R, N), F32),
        compiler_params=_cparams(("parallel",)), name="mod_vectors",
    )(c16, w, b.reshape(1, N))


def _proj_body(a_ref, w_ref, o_ref):
    o_ref[...] = _dot(a_ref[...], w_ref[...]).astype(o_ref.dtype)


def _proj_rope_body(a_ref, w_ref, cos_ref, sin_ref, o_ref):
    acc = _dot(a_ref[...], w_ref[...])
    cos, sin = cos_ref[...], sin_ref[...]
    for j in range(acc.shape[1] // LANES):
        blk = acc[:, j * LANES:(j + 1) * LANES]
        o_ref[:, j * LANES:(j + 1) * LANES] = _rope128(blk, cos, sin).astype(o_ref.dtype)


def project(h, w, tn, rope=None):
    B, n, K = h.shape
    N = w.shape[1]
    tm = min(n, 512)
    nt = n // tm
    a = h.reshape(B * n, K)
    in_specs = [pl.BlockSpec((tm, K), lambda j, i: (i, 0)),
                pl.BlockSpec((K, tn), lambda j, i: (0, j))]
    args = [a, w]
    body = _proj_body
    if rope is not None:
        tab = pl.BlockSpec((tm, LANES), lambda j, i: (i % nt, 0))
        in_specs += [tab, tab]
        args += list(rope)
        body = _proj_rope_body
    out = pl.pallas_call(
        body, grid=(N // tn, (B * n) // tm), in_specs=in_specs,
        out_specs=pl.BlockSpec((tm, tn), lambda j, i: (i, j)),
        out_shape=jax.ShapeDtypeStruct((B * n, N), BF16),
        compiler_params=_cparams(("parallel", "parallel")), name="project",
    )(*args)
    return out.reshape(B, n, N)


def _softmax_parts(scores, extra=None):
    m = scores[0].max(axis=-1, keepdims=True)
    for s in scores[1:]:
        m = jnp.maximum(m, s.max(axis=-1, keepdims=True))
    if extra is not None:
        m = jnp.maximum(m, extra)
    es = [jnp.exp(s - m) for s in scores]
    l = es[0].sum(axis=-1, keepdims=True)
    for e in es[1:]:
        l = l + e.sum(axis=-1, keepdims=True)
    if extra is not None:
        l = l + jnp.exp(extra - m)
    return es, 1.0 / l


def _half_mask(shape, half):
    lane = lax.broadcasted_iota(jnp.int32, shape, 1)
    return (lane < 64) if half == 0 else (lane >= 64)


def _diff_attn_body(lam_ref, *refs, with_x, post_scale):
    if with_x:
        q1_ref, q2_ref, k1x_ref, k2x_ref, vx_ref, k1c_ref, k2c_ref, vc_ref, g_ref, o_ref = refs
    else:
        q1_ref, q2_ref, k1c_ref, k2c_ref, vc_ref, g_ref, o_ref = refs
    lam = lam_ref[0]
    g = g_ref[...]
    scale = A_QK ** -0.5
    for h in range(A_HEADS):
        pair = slice((h // 2) * LANES, (h // 2 + 1) * LANES)
        vs = slice(h * A_V, (h + 1) * A_V)

        def probs(q_ref, kx_ref, kc_ref):
            q = q_ref[:, pair]
            q = jnp.where(_half_mask(q.shape, h % 2), q, jnp.zeros_like(q)) * scale
            scores = [_dot_nt(q, kx_ref[:, pair])] if with_x else []
            scores.append(_dot_nt(q, kc_ref[:, pair]))
            es, rl = _softmax_parts(scores)
            return es, rl

        if with_x:
            e1, rl1 = probs(q1_ref, k1x_ref, k1c_ref)
            e2, rl2 = probs(q2_ref, k2x_ref, k2c_ref)
        else:
            e1, rl1 = probs(q1_ref, None, k1c_ref)
            e2, rl2 = probs(q2_ref, None, k2c_ref)
        rl2 = rl2 * lam
        vals = [vx_ref[:, vs], vc_ref[:, vs]] if with_x else [vc_ref[:, vs]]
        o = None
        for a, b, v in zip(e1, e2, vals):
            p = (a * rl1 - b * rl2).astype(BF16)
            t = _dot(p, v)
            o = t if o is None else o + t
        o_ref[:, vs] = (_rms(o, g) * post_scale).astype(o_ref.dtype)


def diff_attention(lam, pr_q, pr_x, pp_x, pr_c, pp_c, norm_g, post_scale, with_x):
    B, nq, _ = pr_q.shape
    m = pr_c.shape[1]
    tq = min(nq, 256)
    qspec = lambda blk: pl.BlockSpec((None, tq, 256), lambda b, i: (b, i, blk))
    in_specs = [pl.BlockSpec(memory_space=pltpu.SMEM), qspec(0), qspec(1)]
    args = [lam, pr_q, pr_q]
    if with_x:
        n = pr_x.shape[1]
        in_specs += [pl.BlockSpec((None, n, 256), lambda b, i: (b, 0, 2)),
                     pl.BlockSpec((None, n, 256), lambda b, i: (b, 0, 3)),
                     pl.BlockSpec((None, n, 512), lambda b, i: (b, 0, 0))]
        args += [pr_x, pr_x, pp_x]
    in_specs += [pl.BlockSpec((None, m, 256), lambda b, i: (b, 0, 2)),
                 pl.BlockSpec((None, m, 256), lambda b, i: (b, 0, 3)),
                 pl.BlockSpec((None, m, 512), lambda b, i: (b, 0, 0)),
                 pl.BlockSpec((1, A_V), lambda b, i: (0, 0))]
    args += [pr_c, pr_c, pp_c, norm_g.reshape(1, A_V)]
    return pl.pallas_call(
        functools.partial(_diff_attn_body, with_x=with_x, post_scale=post_scale),
        grid=(B, nq // tq), in_specs=in_specs,
        out_specs=pl.BlockSpec((None, tq, 512), lambda b, i: (b, i, 0)),
        out_shape=jax.ShapeDtypeStruct((B, nq, 512), BF16),
        compiler_params=_cparams(("parallel", "parallel")), name="diff_attention",
    )(*args)


def _mla_up_body(cq_ref, ckv_ref, qg_ref, kvg_ref, wq_ref, wkv_ref, *refs, rope):
    if rope:
        cos_ref, sin_ref, q_ref, kv_ref = refs
    else:
        q_ref, kv_ref = refs
    cq = _rms(cq_ref[...].astype(F32), qg_ref[...]).astype(BF16)
    ckv = _rms(ckv_ref[...].astype(F32), kvg_ref[...]).astype(BF16)
    kv_ref[...] = _dot(ckv, wkv_ref[...]).astype(kv_ref.dtype)
    q = _dot(cq, wq_ref[...])
    for h in range(B_HEADS):
        nope = slice(h * 256, h * 256 + LANES)
        ropes = slice(h * 256 + LANES, (h + 1) * 256)
        q_ref[:, nope] = q[:, nope].astype(q_ref.dtype)
        blk = q[:, ropes]
        if rope:
            blk = _rope128(blk, cos_ref[...], sin_ref[...])
        q_ref[:, ropes] = blk.astype(q_ref.dtype)


def mla_up(pp, qn_g, kvn_g, w_uq, w_ukv, rope):
    B, n, _ = pp.shape
    tm = min(n, 512)
    in_specs = [pl.BlockSpec((None, tm, 256), lambda b, i: (b, i, 28)),
                pl.BlockSpec((None, tm, 128), lambda b, i: (b, i, 58)),
                pl.BlockSpec((1, 256), lambda b, i: (0, 0)),
                pl.BlockSpec((1, 128), lambda b, i: (0, 0)),
                pl.BlockSpec((256, 1024), lambda b, i: (0, 0)),
                pl.BlockSpec((128, 1024), lambda b, i: (0, 0))]
    args = [pp, pp, qn_g.reshape(1, 256), kvn_g.reshape(1, 128), w_uq, w_ukv]
    if rope is not None:
        tab = pl.BlockSpec((tm, LANES), lambda b, i: (i, 0))
        in_specs += [tab, tab]
        args += list(rope)
    out = pl.BlockSpec((None, tm, 1024), lambda b, i: (b, i, 0))
    shp = jax.ShapeDtypeStruct((B, n, 1024), BF16)
    return pl.pallas_call(
        functools.partial(_mla_up_body, rope=rope is not None), grid=(B, n // tm),
        in_specs=in_specs, out_specs=[out, out], out_shape=[shp, shp],
        compiler_params=_cparams(("parallel", "parallel")), name="mla_up",
    )(*args)


def _mla_attn_body(*refs, with_x):
    if with_x:
        q_ref, knx_ref, vx_ref, krx_ref, knc_ref, vc_ref, krc_ref, o_ref = refs
    else:
        q_ref, knc_ref, vc_ref, krc_ref, o_ref = refs
    scale = (B_NOPE + B_ROPE) ** -0.5
    for h in range(B_HEADS):
        hs = slice(h * LANES, (h + 1) * LANES)
        qn = q_ref[:, h * 256:h * 256 + LANES]
        qr = q_ref[:, h * 256 + LANES:(h + 1) * 256]
        scores = []
        if with_x:
            scores.append((_dot_nt(qn, knx_ref[:, hs]) + _dot_nt(qr, krx_ref[...])) * scale)
        scores.append((_dot_nt(qn, knc_ref[:, hs]) + _dot_nt(qr, krc_ref[...])) * scale)
        es, rl = _softmax_parts(scores)
        vals = [vx_ref[:, hs], vc_ref[:, hs]] if with_x else [vc_ref[:, hs]]
        o = None
        for e, v in zip(es, vals):
            t = _dot((e * rl).astype(BF16), v)
            o = t if o is None else o + t
        o_ref[:, hs] = o.astype(o_ref.dtype)


def mla_attention(qb, kvb_x, pr_x, kvb_c, pr_c, with_x):
    B, nq, _ = qb.shape
    m = kvb_c.shape[1]
    tq = min(nq, 256)
    in_specs = [pl.BlockSpec((None, tq, 1024), lambda b, i: (b, i, 0))]
    args = [qb]
    if with_x:
        n = kvb_x.shape[1]
        in_specs += [pl.BlockSpec((None, n, 512), lambda b, i: (b, 0, 0)),
                     pl.BlockSpec((None, n, 512), lambda b, i: (b, 0, 1)),
                     pl.BlockSpec((None, n, 128), lambda b, i: (b, 0, 13))]
        args += [kvb_x, kvb_x, pr_x]
    in_specs += [pl.BlockSpec((None, m, 512), lambda b, i: (b, 0, 0)),
                 pl.BlockSpec((None, m, 512), lambda b, i: (b, 0, 1)),
                 pl.BlockSpec((None, m, 128), lambda b, i: (b, 0, 13))]
    args += [kvb_c, kvb_c, pr_c]
    return pl.pallas_call(
        functools.partial(_mla_attn_body, with_x=with_x), grid=(B, nq // tq),
        in_specs=in_specs, out_specs=pl.BlockSpec((None, tq, 512), lambda b, i: (b, i, 0)),
        out_shape=jax.ShapeDtypeStruct((B, nq, 512), BF16),
        compiler_params=_cparams(("parallel", "parallel")), name="mla_attention",
    )(*args)


def _cumsum_rows(tri, g):
    g1 = g.astype(BF16)
    r1 = g - g1.astype(F32)
    g2 = r1.astype(BF16)
    g3 = (r1 - g2.astype(F32)).astype(BF16)
    return _dot(tri, g1) + _dot(tri, g2) + _dot(tri, g3)


def _hgrn_body(qf_ref, zf_ref, if_ref, qb_ref, zb_ref, ib_ref, lb_ref, s0_ref, of_ref, ob_ref, s_ref):
    c = pl.program_id(1)
    L = qf_ref.shape[0]

    @pl.when(c == 0)
    def _():
        s_ref[...] = s0_ref[...]

    row = lax.broadcasted_iota(jnp.int32, (L, L), 0)
    col = lax.broadcasted_iota(jnp.int32, (L, L), 1)
    for d, (q_ref, z_ref, i_ref, o_ref) in enumerate(((qf_ref, zf_ref, if_ref, of_ref),
                                                      (qb_ref, zb_ref, ib_ref, ob_ref))):
        keep = (col <= row) if d == 0 else (col >= row)
        tri = jnp.where(keep, 1.0, 0.0).astype(BF16)
        last = L - 1 if d == 0 else 0
        for h in range(C_HEADS):
            hs = slice(h * C_DK, (h + 1) * C_DK)
            lbh = lb_ref[d:d + 1, hs]
            f = lbh + (1.0 - lbh) * _sigmoid(z_ref[:, hs].astype(F32))
            key = 1.0 - f
            b = _cumsum_rows(tri, jnp.log(f))
            ref = b[L // 2:L // 2 + 1, :]
            bend = b[last:last + 1, :]
            q = q_ref[:, hs].astype(F32)
            v = i_ref[:, hs]
            att = _dot_nt((q * jnp.exp(b - ref)).astype(BF16), (key * jnp.exp(ref - b)).astype(BF16))
            att = jnp.where(keep, att, 0.0).astype(BF16)
            st = s_ref[d * C_HEADS + h]
            o = _dot_nt((q * jnp.exp(b)).astype(BF16), st.astype(BF16)) + _dot(att, v)
            o_ref[:, hs] = o
            kdec = (key * jnp.exp(bend - b)).astype(BF16)
            vt = v.astype(F32).T.astype(BF16)
            s_ref[d * C_HEADS + h] = jnp.exp(bend) * st + _dot(vt, kdec)


def hgrn_scan(pp, lb2, s0):
    B, n, _ = pp.shape
    L = SCAN_CHUNK
    nc = n // L
    fwd = lambda blk: pl.BlockSpec((None, L, 512), lambda b, c: (b, c, blk))
    bwd = lambda blk: pl.BlockSpec((None, L, 512), lambda b, c: (b, nc - 1 - c, blk))
    st = pl.BlockSpec((None, 2 * C_HEADS, C_DV, C_DK), lambda b, c: (b, 0, 0, 0))
    oshape = jax.ShapeDtypeStruct((B, n, 512), F32)
    return pl.pallas_call(
        _hgrn_body, grid=(B, nc),
        in_specs=[fwd(1), fwd(2), fwd(4), bwd(1), bwd(3), bwd(4),
                  pl.BlockSpec((2, 512), lambda b, c: (0, 0)), st],
        out_specs=[fwd(0), bwd(0), st],
        out_shape=[oshape, oshape, jax.ShapeDtypeStruct(s0.shape, F32)],
        compiler_params=_cparams(("parallel", "arbitrary")), name="hgrn_scan",
    )(pp, pp, pp, pp, pp, pp, lb2, s0)


def _place_half(x, src_half, dst_half):
    return x if src_half == dst_half else pltpu.roll(x, 64, 1)


def _window_body(sink_ref, q_ref, *refs, local):
    if local:
        band_ref, kp_ref, ko_ref, kn_ref, vp_ref, vo_ref, vn_ref, kc_ref, vc_ref, o_ref = refs
    else:
        kc_ref, vc_ref, o_ref = refs
    tq = q_ref.shape[0]
    G = D_HEADS // D_KV_HEADS
    if local:
        W = WINDOW
        k_loc = jnp.concatenate([kp_ref[...], ko_ref[...], kn_ref[...]], axis=0)
        v_loc = jnp.concatenate([vp_ref[...], vo_ref[...], vn_ref[...]], axis=0)
        band = band_ref[...]
        valid = jnp.concatenate([band] * G, axis=0) > 0.5
    kc, vc = kc_ref[...], vc_ref[...]
    lo = _half_mask((tq, LANES), 0)
    for kh in range(D_KV_HEADS):
        tiles, sinks = [], []
        for g in range(G):
            hq = kh * G + g
            qp = q_ref[:, (hq // 2) * LANES:(hq // 2 + 1) * LANES].astype(F32)
            q = _place_half(qp, hq % 2, kh)
            tiles.append((jnp.where(_half_mask(q.shape, kh), q, 0.0) * (D_HD ** -0.5)).astype(BF16))
            sinks.append(jnp.full((tq, 1), sink_ref[hq], F32))
        q = jnp.concatenate(tiles, axis=0)
        scores = []
        if local:
            scores.append(jnp.where(valid, _dot_nt(q, k_loc), MASK_VALUE))
        scores.append(_dot_nt(q, kc))
        es, rl = _softmax_parts(scores, extra=jnp.concatenate(sinks, axis=0))
        vals = [v_loc, vc] if local else [vc]
        r = None
        for e, v in zip(es, vals):
            t = _dot((e * rl).astype(BF16), v)
            r = t if r is None else r + t
        outs = [_place_half(r[g * tq:(g + 1) * tq], kh, g % 2) for g in range(G)]
        for sub in range(G // 2):
            pair = kh * (G // 2) + sub
            o_ref[:, pair * LANES:(pair + 1) * LANES] = jnp.where(lo, outs[2 * sub], outs[2 * sub + 1]).astype(o_ref.dtype)


def _band_masks(nb):
    W = WINDOW
    a = jnp.arange(W)[:, None]
    j = jnp.arange(3 * W)[None, :]
    near = jnp.abs(j - W - a) <= W
    masks = []
    for i in (0, 1, nb - 1) if nb > 1 else (0, 0, 0):
        kblk = i + j // W - 1
        masks.append(near & (kblk >= 0) & (kblk < nb))
    return jnp.stack(masks).astype(F32)


def window_attention(sink, pr_q, pp_x, pr_c, pp_c, local):
    B, nq, _ = pr_q.shape
    m = pr_c.shape[1]
    tq = WINDOW if local else min(nq, 256)
    nb = nq // tq
    in_specs = [pl.BlockSpec(memory_space=pltpu.SMEM),
                pl.BlockSpec((None, tq, 512), lambda b, i: (b, i, 2))]
    args = [sink, pr_q]
    if local:
        prev = lambda b, i: jnp.maximum(i - 1, 0)
        nxt = lambda b, i: jnp.minimum(i + 1, nb - 1)
        edge = lambda b, i: jnp.where(i == 0, 0, jnp.where(i == nb - 1, 2, 1))
        in_specs.append(pl.BlockSpec((None, tq, 3 * tq), lambda b, i: (edge(b, i), 0, 0)))
        args.append(_band_masks(nb))
        for blk, arr in ((12, pr_q), (59, pp_x)):
            in_specs += [pl.BlockSpec((None, tq, 128), lambda b, i, blk=blk: (b, prev(b, i), blk)),
                         pl.BlockSpec((None, tq, 128), lambda b, i, blk=blk: (b, i, blk)),
                         pl.BlockSpec((None, tq, 128), lambda b, i, blk=blk: (b, nxt(b, i), blk))]
            args += [arr, arr, arr]
    in_specs += [pl.BlockSpec((None, m, 128), lambda b, i: (b, 0, 12)),
                 pl.BlockSpec((None, m, 128), lambda b, i: (b, 0, 59))]
    args += [pr_c, pp_c]
    return pl.pallas_call(
        functools.partial(_window_body, local=local), grid=(B, nb), in_specs=in_specs,
        out_specs=pl.BlockSpec((None, tq, 512), lambda b, i: (b, i, 0)),
        out_shape=jax.ShapeDtypeStruct((B, nq, 512), BF16),
        compiler_params=_cparams(("parallel", "parallel")), name="window_attention",
    )(*args)


def _merge_body(oa_ref, ob_ref, of_ref, obk_ref, hg_ref, hng_ref, od_ref, g0_ref, g1_ref, g2_ref, g3_ref,
                wb_ref, wo_ref, x_ref, gate_ref, o_ref):
    hng = hng_ref[...]
    oc = of_ref[...] + obk_ref[...]
    parts = []
    for h in range(C_HEADS):
        hs = slice(h * C_DV, (h + 1) * C_DV)
        g = hg_ref[:, hs].astype(F32)
        parts.append((_rms(oc[:, hs], hng) * (g * _sigmoid(g))).astype(BF16))
    oh = jnp.concatenate(parts, axis=1)
    y = None
    branches = (oa_ref[...], ob_ref[...], oh, od_ref[...])
    for j, (br, gl_ref) in enumerate(zip(branches, (g0_ref, g1_ref, g2_ref, g3_ref))):
        term = _sigmoid(gl_ref[...].astype(F32)) * _dot(br, wb_ref[j])
        y = term if y is None else y + term
    o_ref[...] = x_ref[...] + gate_ref[...] * _dot(y.astype(BF16), wo_ref[...])


def merge(oa, ob, of, obk, pp, hgrn_g, od, w_branch, w_out, x, gate):
    B, n, D = x.shape
    tm = min(n, 256)
    blk = lambda w, k: pl.BlockSpec((None, tm, w), lambda b, i: (b, i, k))
    in_specs = [blk(512, 0), blk(512, 0), blk(512, 0), blk(512, 0), blk(512, 5),
                pl.BlockSpec((1, C_DV), lambda b, i: (0, 0)), blk(512, 0),
                blk(1024, 3), blk(1024, 4), blk(1024, 5), blk(1024, 6),
                pl.BlockSpec((N_BRANCH, BR_W, D), lambda b, i: (0, 0, 0)),
                pl.BlockSpec((D, D), lambda b, i: (0, 0)),
                blk(D, 0), pl.BlockSpec((None, 1, D), lambda b, i: (b, 0, 0))]
    return pl.pallas_call(
        _merge_body, grid=(B, n // tm), in_specs=in_specs, out_specs=blk(D, 0),
        out_shape=jax.ShapeDtypeStruct((B, n, D), F32),
        compiler_params=_cparams(("parallel", "parallel")), name="merge",
    )(oa, ob, of, obk, pp, hgrn_g.reshape(1, C_DV), od, pp, pp, pp, pp, w_branch, w_out, x, gate)


N_SORTED = PEER_TOPK + 1
GROUP_KEYS = 2
PAIR_CANDS = [(a, b) for a in range(N_SORTED) for b in range(N_SORTED) if (a + 1) * (b + 1) <= N_SORTED]


def _peer_route_body(x_ref, g_ref, sh_ref, sc_ref, wqt_ref, kcat_ref,
                     h_ref, cnt_ref, e1_ref, rk_ref, e2_ref, work_ref, sv_ref):
    tb = x_ref.shape[0]
    H, NK = PEER_HEADS, N_KEYS
    y = _rms(x_ref[...], g_ref[...])
    hmod = (y * (1.0 + sc_ref[...]) + sh_ref[...]).astype(BF16)
    h_ref[...] = hmod
    qt = _dot_nt(wqt_ref[...], hmod).astype(BF16)
    s_all = _dot(kcat_ref[...], qt)
    s1 = s_all[0:NK * H].reshape(NK, H, tb)
    s2 = s_all[NK * H:2 * NK * H].reshape(NK, H, tb)
    s2h = s_all[2 * NK * H:3 * NK * H]

    work_ref[0] = s1
    work_ref[1] = s2

    def body(r, prev):
        nxt = []
        for p in range(2):
            w = work_ref[p]
            mx = jnp.max(jnp.where(w < prev[p][None], w, -jnp.inf), axis=0)
            sv_ref[p, r] = mx
            nxt.append(mx)
        return tuple(nxt)

    inf = jnp.full((H, tb), jnp.inf, F32)
    lax.fori_loop(0, N_SORTED, body, (inf, inf))

    sv1 = [sv_ref[0, a] for a in range(N_SORTED)]
    sv2 = [sv_ref[1, b] for b in range(N_SORTED)]
    cands = [sv1[a] + sv2[b] for a, b in PAIR_CANDS]
    cur = list(cands)
    t16 = t17 = None
    for r in range(N_SORTED):
        mx = functools.reduce(jnp.maximum, cur)
        if r == PEER_TOPK - 1:
            t16 = mx
        if r == PEER_TOPK:
            t17 = mx
        cur = [jnp.where(cc == mx, -jnp.inf, cc) for cc in cur]
    tau = 0.5 * (t16 + t17)
    top = sv1[0] + sv2[0]
    z = functools.reduce(lambda u, w: u + w, [jnp.where(cc >= tau, jnp.exp(cc - top), 0.0) for cc in cands])
    rz = 1.0 / z
    need = tau[None] - s1
    cnt = None
    for b in range(PEER_TOPK):
        hit = jnp.where(sv2[b][None] >= need, 1.0, 0.0)
        cnt = hit if cnt is None else cnt + hit
    cnt_ref[...] = cnt.reshape(NK * H, tb)
    e1_ref[...] = (jnp.exp(s1 - sv1[0][None]) * rz[None]).reshape(NK * H, tb)
    for h in range(H):
        rows = slice(h * NK, (h + 1) * NK)
        sh = s2h[rows, :]
        rank = None
        for b in range(PEER_TOPK):
            above = jnp.where(sv2[b][h:h + 1, :] > sh, 1.0, 0.0)
            rank = above if rank is None else rank + above
        rk_ref[rows, :] = rank.astype(BF16)
        e2_ref[rows, :] = jnp.exp(sh - sv2[0][h:h + 1, :]).astype(BF16)


def peer_route(x, g, shift, scale, wqt, kcat):
    B, n, D = x.shape
    tb = min(n, 256)
    nt = n // tb
    N = B * n
    R = N_KEYS * PEER_HEADS
    xf = x.reshape(N, D)
    vec = pl.BlockSpec((None, 1, D), lambda t: (t // nt, 0, 0))
    col = pl.BlockSpec((R, tb), lambda t: (0, t))
    cshape = jax.ShapeDtypeStruct((R, N), F32)
    hshape = jax.ShapeDtypeStruct((R, N), BF16)
    return pl.pallas_call(
        _peer_route_body, grid=(N // tb,),
        in_specs=[pl.BlockSpec((tb, D), lambda t: (t, 0)), pl.BlockSpec((1, D), lambda t: (0, 0)), vec, vec,
                  pl.BlockSpec((D, D), lambda t: (0, 0)), pl.BlockSpec((3 * R, D), lambda t: (0, 0))],
        out_specs=[pl.BlockSpec((tb, D), lambda t: (t, 0)), col, col, col, col],
        out_shape=[jax.ShapeDtypeStruct((N, D), BF16), cshape, cshape, hshape, hshape],
        scratch_shapes=[pltpu.VMEM((2, N_KEYS, PEER_HEADS, tb), F32),
                        pltpu.VMEM((2, N_SORTED, PEER_HEADS, tb), F32)],
        compiler_params=_cparams(("parallel",)), name="peer_route",
    )(xf, g.reshape(1, D), shift, scale, wqt, kcat)


def _gelu(x):
    return 0.5 * x * (1.0 + lax.erf(x * math.sqrt(0.5)))


def _peer_dense_body(h_ref, u_ref, vt_ref, cnt_ref, e1_ref, rk_ref, e2_ref, x_ref, gate_ref, o_ref,
                     acc_ref, act_a, act_b, p_a, p_b, rk_s, e2_s, *, groups):
    e = pl.program_id(1)
    H, NK, GI = PEER_HEADS, N_KEYS, GROUP_KEYS
    tb = h_ref.shape[0]
    acts, ps = (act_a, act_b), (p_a, p_b)

    @pl.when(e == 0)
    def _():
        acc_ref[...] = jnp.zeros_like(acc_ref)
        rk_s[...] = rk_ref[...]
        e2_s[...] = e2_ref[...]

    def mxu_act(g, slot):
        acts[slot][...] = _dot_nt(u_ref[g], h_ref[...])

    def mxu_out(g, slot):
        acc_ref[...] += _dot(vt_ref[g], ps[slot][...])

    def vpu(g, slot):
        i0 = (e * groups + g) * GI
        for c in range(tb // LANES):
            cs = slice(c * LANES, (c + 1) * LANES)
            rows8 = [pl.ds(pl.multiple_of((i0 + k) * H, SUBLANES), H) for k in range(GI)]
            cnt8 = [cnt_ref[r, cs] for r in rows8]
            e18 = [e1_ref[r, cs] for r in rows8]
            w = [None] * GI
            for h in range(H):
                hr = slice(h * NK, (h + 1) * NK)
                rkt, e2t = rk_s[hr, cs], e2_s[hr, cs]
                for k in range(GI):
                    thr = jnp.broadcast_to(cnt8[k][h:h + 1, :], (NK, LANES)).astype(BF16)
                    wgt = jnp.broadcast_to(e18[k][h:h + 1, :], (NK, LANES)).astype(BF16)
                    term = jnp.where(rkt < thr, e2t, jnp.zeros_like(e2t)) * wgt
                    w[k] = term if w[k] is None else w[k] + term
            for k in range(GI):
                rows = slice(k * NK, (k + 1) * NK)
                ps[slot][rows, cs] = _gelu(acts[slot][rows, cs]).astype(BF16) * w[k]

    mxu_act(0, 0)
    vpu(0, 0)
    mxu_act(1, 1)

    def pair(k, carry):
        g = 2 * k + 1
        vpu(g, 1)
        mxu_act(g + 1, 0)
        mxu_out(g - 1, 0)
        vpu(g + 1, 0)
        mxu_act(g + 2, 1)
        mxu_out(g, 1)
        return carry

    lax.fori_loop(0, (groups - 2) // 2, pair, 0)
    vpu(groups - 1, 1)
    mxu_out(groups - 2, 0)
    mxu_out(groups - 1, 1)

    @pl.when(e == pl.num_programs(1) - 1)
    def _():
        o_ref[...] = x_ref[...] + gate_ref[...] * acc_ref[...].T


def peer_dense(hmod, u3, vt3, cnt, e1, rk, e2, x, gate):
    B, n, D = x.shape
    N = B * n
    tb = min(n, 512)
    nt = n // tb
    groups = 8
    ge = GROUP_KEYS * N_KEYS
    R = N_KEYS * PEER_HEADS
    col = pl.BlockSpec((R, tb), lambda t, e: (0, t))
    out = pl.pallas_call(
        functools.partial(_peer_dense_body, groups=groups), grid=(N // tb, u3.shape[0] // groups),
        in_specs=[pl.BlockSpec((tb, D), lambda t, e: (t, 0)),
                  pl.BlockSpec((groups, ge, D), lambda t, e: (e, 0, 0)),
                  pl.BlockSpec((groups, D, ge), lambda t, e: (e, 0, 0)),
                  col, col, col, col,
                  pl.BlockSpec((tb, D), lambda t, e: (t, 0)),
                  pl.BlockSpec((None, 1, D), lambda t, e: (t // nt, 0, 0))],
        out_specs=pl.BlockSpec((tb, D), lambda t, e: (t, 0)),
        out_shape=jax.ShapeDtypeStruct((N, D), F32),
        scratch_shapes=[pltpu.VMEM((D, tb), F32), pltpu.VMEM((ge, tb), F32), pltpu.VMEM((ge, tb), F32),
                        pltpu.VMEM((ge, tb), BF16), pltpu.VMEM((ge, tb), BF16),
                        pltpu.VMEM((R, tb), BF16), pltpu.VMEM((R, tb), BF16)],
        compiler_params=_cparams(("parallel", "arbitrary")), name="peer_dense",
    )(hmod, u3, vt3, cnt, e1, rk, e2, x.reshape(N, D), gate)
    return out.reshape(B, n, D)


def _rope_tables(n):
    rows = n // GRID_W
    r = jnp.repeat(jnp.arange(rows, dtype=F32), GRID_W)
    col = jnp.tile(jnp.arange(GRID_W, dtype=F32), rows)
    m = ROPE_DIM // 2
    freqs = ROPE_BASE ** (-2.0 * jnp.arange(m // 2, dtype=F32) / m)
    ar, ac = r[:, None] * freqs, col[:, None] * freqs
    cos = jnp.concatenate([jnp.cos(ar), jnp.cos(ar), jnp.cos(ac), jnp.cos(ac)], axis=-1)
    sin = jnp.concatenate([-jnp.sin(ar), jnp.sin(ar), -jnp.sin(ac), jnp.sin(ac)], axis=-1)
    return jnp.tile(cos, (1, 2)), jnp.tile(sin, (1, 2))


def _prep_w_in(w):
    sizes = (256, 256, 256, 256, 512, 256, 128, 64, 512, 512, 512, 512, 512, 512, 128, 128, 4096)
    offs = np.concatenate([[0], np.cumsum(sizes)])
    piece = lambda k: w[:, offs[k]:offs[k + 1]]
    zeros = jnp.zeros((w.shape[0], 64), w.dtype)
    w_rope = jnp.concatenate([piece(0), piece(1), piece(2), piece(3), piece(13), piece(14), piece(7), zeros], axis=1)
    w_plain = jnp.concatenate([piece(4), piece(8), piece(9), piece(10), piece(11), piece(12), piece(16),
                               piece(5), piece(6), piece(15)], axis=1)
    return w_rope.astype(BF16), w_plain.astype(BF16)


def _prep_mla(w_uq, w_ukv):
    wq = w_uq.reshape(B_Q_LORA, B_HEADS, B_NOPE + B_ROPE)
    wq = jnp.pad(wq, ((0, 0), (0, 0), (0, 256 - B_NOPE - B_ROPE))).reshape(B_Q_LORA, B_HEADS * 256)
    wkv = w_ukv.reshape(B_KV_LORA, B_HEADS, B_NOPE + B_V)
    wkv = jnp.concatenate([wkv[:, :, :B_NOPE].reshape(B_KV_LORA, -1), wkv[:, :, B_NOPE:].reshape(B_KV_LORA, -1)], axis=1)
    return wq.astype(BF16), wkv.astype(BF16)


def _prep_peer_keys(keys):
    H, NK, hd = PEER_HEADS, N_KEYS, PEER_DK // 2
    eye = jnp.eye(H, dtype=keys.dtype)

    def place(p, head_major):
        k = keys[:, p]
        sel = jnp.zeros((2,), keys.dtype).at[p].set(1.0)
        full = jnp.einsum('hkd,hg,p->khgpd', k, eye, sel)
        if head_major:
            full = full.transpose(1, 0, 2, 3, 4)
        return full.reshape(NK * H, H * 2 * hd)

    return jnp.concatenate([place(0, False), place(1, False), place(1, True)], axis=0).astype(BF16)


def kernel(x, c, ctx, c_ctx, w_mod, b_mod, norm1_g, norm2_g, w_in, diff_lam_q1, diff_lam_k1, diff_lam_q2,
           diff_lam_k2, diff_norm_g, mla_qnorm_g, mla_kvnorm_g, mla_w_uq, mla_w_ukv, hgrn_lb, hgrn_norm_g,
           win_sink, w_branch, w_out, peer_wq, peer_keys, peer_u, peer_v, final_g):
    B, n, D = x.shape
    depth = w_mod.shape[0]
    rope = _rope_tables(n)
    lb_p = jax.nn.softmax(hgrn_lb.astype(F32), axis=1)
    lb = jnp.cumsum(lb_p, axis=1) - lb_p[:, :1]
    c16 = jnp.concatenate([c, c_ctx[None], jnp.zeros((16 - B - 1, D), F32)], axis=0)
    xc = ctx
    for l in range(depth):
        last = l == depth - 1
        mod = mod_vectors(c16, w_mod[l].astype(BF16), b_mod[l])
        mx = [mod[:B, k * D:(k + 1) * D][:, None, :] for k in range(6)]
        mc = [jnp.broadcast_to(mod[B, k * D:(k + 1) * D][None, None, :], (B, 1, D)) for k in range(6)]
        w_rope, w_plain = _prep_w_in(w_in[l])
        w_uq, w_ukv = _prep_mla(mla_w_uq[l], mla_w_ukv[l])
        wb, wo = w_branch[l].astype(BF16), w_out[l].astype(BF16)

        hx = norm_mod(x, norm1_g[l], mx[0], mx[1])
        hc = norm_mod(xc, norm1_g[l], mc[0], mc[1])
        pr_x = project(hx, w_rope, 896, rope)
        pp_x = project(hx, w_plain, 1536)
        pr_c = project(hc, w_rope, 896)
        pp_c = project(hc, w_plain, 1536)

        lam_init = 0.8 - 0.6 * math.exp(-0.3 * l)
        lam = (jnp.exp(jnp.sum(diff_lam_q1[l] * diff_lam_k1[l])) - jnp.exp(jnp.sum(diff_lam_q2[l] * diff_lam_k2[l]))
               + lam_init).reshape(1).astype(F32)
        qb_x, kvb_x = mla_up(pp_x, mla_qnorm_g[l], mla_kvnorm_g[l], w_uq, w_ukv, rope)
        qb_c, kvb_c = mla_up(pp_c, mla_qnorm_g[l], mla_kvnorm_g[l], w_uq, w_ukv, None)
        lb2 = lb[:, l, :]
        s0 = jnp.zeros((B, 2 * C_HEADS, C_DV, C_DK), F32)
        of_c, ob_c, s_ctx = hgrn_scan(pp_c, lb2, s0)
        sink = win_sink[l].astype(F32)

        oa = diff_attention(lam, pr_x, pr_x, pp_x, pr_c, pp_c, diff_norm_g[l], 1.0 - lam_init, True)
        ob = mla_attention(qb_x, kvb_x, pr_x, kvb_c, pr_c, True)
        of_x, ob_x, _ = hgrn_scan(pp_x, lb2, s_ctx)
        od = window_attention(sink, pr_x, pp_x, pr_c, pp_c, True)
        x_new = merge(oa, ob, of_x, ob_x, pp_x, hgrn_norm_g[l], od, wb, wo, x, mx[2])
        if not last:
            oa_c = diff_attention(lam, pr_c, None, None, pr_c, pp_c, diff_norm_g[l], 1.0 - lam_init, False)
            ob_c2 = mla_attention(qb_c, None, None, kvb_c, pr_c, False)
            od_c = window_attention(sink, pr_c, None, pr_c, pp_c, False)
            xc = merge(oa_c, ob_c2, of_c, ob_c, pp_c, hgrn_norm_g[l], od_c, wb, wo, xc, mc[2])
        x = x_new

        wqt = peer_wq[l].T.astype(BF16)
        kcat = _prep_peer_keys(peer_keys[l])
        ge = GROUP_KEYS * N_KEYS
        u3 = peer_u[l].astype(BF16).reshape(-1, ge, D)
        vt3 = peer_v[l].astype(BF16).reshape(-1, ge, D).transpose(0, 2, 1)
        routed = peer_route(x, norm2_g[l], mx[3], mx[4], wqt, kcat)
        x = peer_dense(routed[0], u3, vt3, *routed[1:], x, mx[5])
        if not last:
            routed = peer_route(xc, norm2_g[l], mc[3], mc[4], wqt, kcat)
            xc = peer_dense(routed[0], u3, vt3, *routed[1:], xc, mc[5])
    return final_norm(x, final_g)
```

```python
import functools
import math

import jax
import jax.numpy as jnp
import numpy as np
from jax import lax
from jax.experimental import pallas as pl
from jax.experimental.pallas import tpu as pltpu

F32 = jnp.float32
BF16 = jnp.bfloat16

D_MODEL = 1024
GRID_W = 64
EPS = 1e-6
ROPE_BASE = 10000.0
ROPE_DIM = 64
MASK_VALUE = -1e30
A_HEADS, A_QK, A_V = 4, 64, 128
B_HEADS, B_Q_LORA, B_KV_LORA, B_NOPE, B_ROPE, B_V = 4, 256, 128, 128, 64, 128
C_HEADS, C_DK, C_DV = 4, 128, 128
D_HEADS, D_KV_HEADS, D_HD, WINDOW = 8, 2, 64, 128
N_BRANCH, BR_W = 4, 512
PEER_HEADS, N_KEYS, PEER_DK, PEER_TOPK = 8, 128, 128, 16

LANES = 128
SUBLANES = 8
VMEM_LIMIT = 56 * 1024 * 1024
SCAN_CHUNK = 64
HGRN_BATCH = 2

ROPE_COLS = 1792
PLAIN_COLS = 7680


def _cparams(sem):
    return pltpu.CompilerParams(dimension_semantics=sem, vmem_limit_bytes=VMEM_LIMIT)


def _dot(a, b):
    return jnp.dot(a, b, preferred_element_type=F32)


def _dot_nt(a, b):
    return lax.dot_general(a, b, (((1,), (1,)), ((), ())), preferred_element_type=F32)


def _sigmoid(x):
    return 1.0 / (1.0 + jnp.exp(-x))


def _rms(x, g):
    return x * lax.rsqrt(jnp.mean(x * x, axis=-1, keepdims=True) + EPS) * g


def _swap16(x):
    lane = lax.broadcasted_iota(jnp.int32, x.shape, 1)
    up = pltpu.roll(x, LANES - 16, 1)
    dn = pltpu.roll(x, 16, 1)
    return jnp.where((lane & 31) < 16, up, dn)


def _rope128(x, cos, sin):
    return x * cos + _swap16(x) * sin


def _norm_mod_body(x_ref, g_ref, sh_ref, sc_ref, o_ref):
    y = _rms(x_ref[...], g_ref[...])
    o_ref[...] = (y * (1.0 + sc_ref[...]) + sh_ref[...]).astype(o_ref.dtype)


def _norm_body(x_ref, g_ref, o_ref):
    o_ref[...] = _rms(x_ref[...], g_ref[...]).astype(o_ref.dtype)


def norm_mod(x, g, shift, scale):
    B, n, D = x.shape
    tm = min(n, 512)
    row = pl.BlockSpec((None, tm, D), lambda b, i: (b, i, 0))
    vec = pl.BlockSpec((None, 1, D), lambda b, i: (b, 0, 0))
    return pl.pallas_call(
        _norm_mod_body, grid=(B, n // tm),
        in_specs=[row, pl.BlockSpec((1, D), lambda b, i: (0, 0)), vec, vec],
        out_specs=row, out_shape=jax.ShapeDtypeStruct((B, n, D), BF16),
        compiler_params=_cparams(("parallel", "parallel")), name="norm_mod",
    )(x, g.reshape(1, D), shift, scale)


def final_norm(x, g):
    B, n, D = x.shape
    tm = min(n, 512)
    row = pl.BlockSpec((None, tm, D), lambda b, i: (b, i, 0))
    return pl.pallas_call(
        _norm_body, grid=(B, n // tm),
        in_specs=[row, pl.BlockSpec((1, D), lambda b, i: (0, 0))],
        out_specs=row, out_shape=jax.ShapeDtypeStruct((B, n, D), x.dtype),
        compiler_params=_cparams(("parallel", "parallel")), name="final_norm",
    )(x, g.reshape(1, D))


def _mod_body(c_ref, w_ref, b_ref, o_ref):
    c = c_ref[...]
    s = (c * _sigmoid(c)).astype(BF16)
    o_ref[...] = _dot(s, w_ref[...]) + b_ref[...]


def mod_vectors(c16, w, b):
    R, D = c16.shape
    N = w.shape[1]
    tn = 1536
    return pl.pallas_call(
        _mod_body, grid=(N // tn,),
        in_specs=[pl.BlockSpec((R, D), lambda j: (0, 0)),
                  pl.BlockSpec((D, tn), lambda j: (0, j)),
                  pl.BlockSpec((1, tn), lambda j: (0, j))],
        out_specs=pl.BlockSpec((R, tn), lambda j: (0, j)),
        out_shape=jax.ShapeDtypeStruct((R, N), F32),
        compiler_params=_cparams(("parallel",)), name="mod_vectors",
    )(c16, w, b.reshape(1, N))


def _proj_body(a_ref, w_ref, o_ref):
    o_ref[...] = _dot(a_ref[...], w_ref[...]).astype(o_ref.dtype)


def _proj_rope_body(a_ref, w_ref, cos_ref, sin_ref, o_ref):
    acc = _dot(a_ref[...], w_ref[...])
    cos, sin = cos_ref[...], sin_ref[...]
    for j in range(acc.shape[1] // LANES):
        blk = acc[:, j * LANES:(j + 1) * LANES]
        o_ref[:, j * LANES:(j + 1) * LANES] = _rope128(blk, cos, sin).astype(o_ref.dtype)


def project(h, w, tn, rope=None):
    B, n, K = h.shape
    N = w.shape[1]
    tm = min(n, 512)
    nt = n // tm
    a = h.reshape(B * n, K)
    in_specs = [pl.BlockSpec((tm, K), lambda j, i: (i, 0)),
                pl.BlockSpec((K, tn), lambda j, i: (0, j))]
    args = [a, w]
    body = _proj_body
    if rope is not None:
        tab = pl.BlockSpec((tm, LANES), lambda j, i: (i % nt, 0))
        in_specs += [tab, tab]
        args += list(rope)
        body = _proj_rope_body
    out = pl.pallas_call(
        body, grid=(N // tn, (B * n) // tm), in_specs=in_specs,
        out_specs=pl.BlockSpec((tm, tn), lambda j, i: (i, j)),
        out_shape=jax.ShapeDtypeStruct((B * n, N), BF16),
        compiler_params=_cparams(("parallel", "parallel")), name="project",
    )(*args)
    return out.reshape(B, n, N)


def _softmax_parts(scores, extra=None):
    m = scores[0].max(axis=-1, keepdims=True)
    for s in scores[1:]:
        m = jnp.maximum(m, s.max(axis=-1, keepdims=True))
    if extra is not None:
        m = jnp.maximum(m, extra)
    es = [jnp.exp(s - m) for s in scores]
    l = es[0].sum(axis=-1, keepdims=True)
    for e in es[1:]:
        l = l + e.sum(axis=-1, keepdims=True)
    if extra is not None:
        l = l + jnp.exp(extra - m)
    return es, 1.0 / l


def _half_mask(shape, half):
    lane = lax.broadcasted_iota(jnp.int32, shape, 1)
    return (lane < 64) if half == 0 else (lane >= 64)


def _diff_attn_body(lam_ref, *refs, with_x, post_scale):
    if with_x:
        q1_ref, q2_ref, k1x_ref, k2x_ref, vx_ref, k1c_ref, k2c_ref, vc_ref, g_ref, o_ref = refs
    else:
        q1_ref, q2_ref, k1c_ref, k2c_ref, vc_ref, g_ref, o_ref = refs
    lam = lam_ref[0]
    g = g_ref[...]
    scale = A_QK ** -0.5
    for h in range(A_HEADS):
        pair = slice((h // 2) * LANES, (h // 2 + 1) * LANES)
        vs = slice(h * A_V, (h + 1) * A_V)

        def probs(q_ref, kx_ref, kc_ref):
            q = q_ref[:, pair]
            q = jnp.where(_half_mask(q.shape, h % 2), q, jnp.zeros_like(q)) * scale
            scores = [_dot_nt(q, kx_ref[:, pair])] if with_x else []
            scores.append(_dot_nt(q, kc_ref[:, pair]))
            es, rl = _softmax_parts(scores)
            return es, rl

        if with_x:
            e1, rl1 = probs(q1_ref, k1x_ref, k1c_ref)
            e2, rl2 = probs(q2_ref, k2x_ref, k2c_ref)
        else:
            e1, rl1 = probs(q1_ref, None, k1c_ref)
            e2, rl2 = probs(q2_ref, None, k2c_ref)
        rl2 = rl2 * lam
        vals = [vx_ref[:, vs], vc_ref[:, vs]] if with_x else [vc_ref[:, vs]]
        o = None
        for a, b, v in zip(e1, e2, vals):
            p = (a * rl1 - b * rl2).astype(BF16)
            t = _dot(p, v)
            o = t if o is None else o + t
        o_ref[:, vs] = (_rms(o, g) * post_scale).astype(o_ref.dtype)


def diff_attention(lam, pr_q, pr_x, pp_x, pr_c, pp_c, norm_g, post_scale, with_x):
    B, nq, _ = pr_q.shape
    m = pr_c.shape[1]
    tq = min(nq, 256)
    qspec = lambda blk: pl.BlockSpec((None, tq, 256), lambda b, i: (b, i, blk))
    in_specs = [pl.BlockSpec(memory_space=pltpu.SMEM), qspec(0), qspec(1)]
    args = [lam, pr_q, pr_q]
    if with_x:
        n = pr_x.shape[1]
        in_specs += [pl.BlockSpec((None, n, 256), lambda b, i: (b, 0, 2)),
                     pl.BlockSpec((None, n, 256), lambda b, i: (b, 0, 3)),
                     pl.BlockSpec((None, n, 512), lambda b, i: (b, 0, 0))]
        args += [pr_x, pr_x, pp_x]
    in_specs += [pl.BlockSpec((None, m, 256), lambda b, i: (b, 0, 2)),
                 pl.BlockSpec((None, m, 256), lambda b, i: (b, 0, 3)),
                 pl.BlockSpec((None, m, 512), lambda b, i: (b, 0, 0)),
                 pl.BlockSpec((1, A_V), lambda b, i: (0, 0))]
    args += [pr_c, pr_c, pp_c, norm_g.reshape(1, A_V)]
    return pl.pallas_call(
        functools.partial(_diff_attn_body, with_x=with_x, post_scale=post_scale),
        grid=(B, nq // tq), in_specs=in_specs,
        out_specs=pl.BlockSpec((None, tq, 512), lambda b, i: (b, i, 0)),
        out_shape=jax.ShapeDtypeStruct((B, nq, 512), BF16),
        compiler_params=_cparams(("parallel", "parallel")), name="diff_attention",
    )(*args)


def _mla_up_body(cq_ref, ckv_ref, qg_ref, kvg_ref, wq_ref, wkv_ref, *refs, rope):
    if rope:
        cos_ref, sin_ref, q_ref, kv_ref = refs
    else:
        q_ref, kv_ref = refs
    cq = _rms(cq_ref[...].astype(F32), qg_ref[...]).astype(BF16)
    ckv = _rms(ckv_ref[...].astype(F32), kvg_ref[...]).astype(BF16)
    kv_ref[...] = _dot(ckv, wkv_ref[...]).astype(kv_ref.dtype)
    q = _dot(cq, wq_ref[...])
    for h in range(B_HEADS):
        nope = slice(h * 256, h * 256 + LANES)
        ropes = slice(h * 256 + LANES, (h + 1) * 256)
        q_ref[:, nope] = q[:, nope].astype(q_ref.dtype)
        blk = q[:, ropes]
        if rope:
            blk = _rope128(blk, cos_ref[...], sin_ref[...])
        q_ref[:, ropes] = blk.astype(q_ref.dtype)


def mla_up(pp, qn_g, kvn_g, w_uq, w_ukv, rope):
    B, n, _ = pp.shape
    tm = min(n, 512)
    in_specs = [pl.BlockSpec((None, tm, 256), lambda b, i: (b, i, 28)),
                pl.BlockSpec((None, tm, 128), lambda b, i: (b, i, 58)),
                pl.BlockSpec((1, 256), lambda b, i: (0, 0)),
                pl.BlockSpec((1, 128), lambda b, i: (0, 0)),
                pl.BlockSpec((256, 1024), lambda b, i: (0, 0)),
                pl.BlockSpec((128, 1024), lambda b, i: (0, 0))]
    args = [pp, pp, qn_g.reshape(1, 256), kvn_g.reshape(1, 128), w_uq, w_ukv]
    if rope is not None:
        tab = pl.BlockSpec((tm, LANES), lambda b, i: (i, 0))
        in_specs += [tab, tab]
        args += list(rope)
    out = pl.BlockSpec((None, tm, 1024), lambda b, i: (b, i, 0))
    shp = jax.ShapeDtypeStruct((B, n, 1024), BF16)
    return pl.pallas_call(
        functools.partial(_mla_up_body, rope=rope is not None), grid=(B, n // tm),
        in_specs=in_specs, out_specs=[out, out], out_shape=[shp, shp],
        compiler_params=_cparams(("parallel", "parallel")), name="mla_up",
    )(*args)


def _mla_attn_body(*refs, with_x):
    if with_x:
        q_ref, knx_ref, vx_ref, krx_ref, knc_ref, vc_ref, krc_ref, o_ref = refs
    else:
        q_ref, knc_ref, vc_ref, krc_ref, o_ref = refs
    scale = (B_NOPE + B_ROPE) ** -0.5
    for h in range(B_HEADS):
        hs = slice(h * LANES, (h + 1) * LANES)
        qn = q_ref[:, h * 256:h * 256 + LANES]
        qr = q_ref[:, h * 256 + LANES:(h + 1) * 256]
        scores = []
        if with_x:
            scores.append((_dot_nt(qn, knx_ref[:, hs]) + _dot_nt(qr, krx_ref[...])) * scale)
        scores.append((_dot_nt(qn, knc_ref[:, hs]) + _dot_nt(qr, krc_ref[...])) * scale)
        es, rl = _softmax_parts(scores)
        vals = [vx_ref[:, hs], vc_ref[:, hs]] if with_x else [vc_ref[:, hs]]
        o = None
        for e, v in zip(es, vals):
            t = _dot((e * rl).astype(BF16), v)
            o = t if o is None else o + t
        o_ref[:, hs] = o.astype(o_ref.dtype)


def mla_attention(qb, kvb_x, pr_x, kvb_c, pr_c, with_x):
    B, nq, _ = qb.shape
    m = kvb_c.shape[1]
    tq = min(nq, 256)
    in_specs = [pl.BlockSpec((None, tq, 1024), lambda b, i: (b, i, 0))]
    args = [qb]
    if with_x:
        n = kvb_x.shape[1]
        in_specs += [pl.BlockSpec((None, n, 512), lambda b, i: (b, 0, 0)),
                     pl.BlockSpec((None, n, 512), lambda b, i: (b, 0, 1)),
                     pl.BlockSpec((None, n, 128), lambda b, i: (b, 0, 13))]
        args += [kvb_x, kvb_x, pr_x]
    in_specs += [pl.BlockSpec((None, m, 512), lambda b, i: (b, 0, 0)),
                 pl.BlockSpec((None, m, 512), lambda b, i: (b, 0, 1)),
                 pl.BlockSpec((None, m, 128), lambda b, i: (b, 0, 13))]
    args += [kvb_c, kvb_c, pr_c]
    return pl.pallas_call(
        functools.partial(_mla_attn_body, with_x=with_x), grid=(B, nq // tq),
        in_specs=in_specs, out_specs=pl.BlockSpec((None, tq, 512), lambda b, i: (b, i, 0)),
        out_shape=jax.ShapeDtypeStruct((B, nq, 512), BF16),
        compiler_params=_cparams(("parallel", "parallel")), name="mla_attention",
    )(*args)


def _cumsum_rows(tri, g):
    g1 = g.astype(BF16)
    r1 = g - g1.astype(F32)
    g2 = r1.astype(BF16)
    g3 = (r1 - g2.astype(F32)).astype(BF16)
    return _dot(tri, g1) + _dot(tri, g2) + _dot(tri, g3)


def _hgrn_body(qf_ref, zf_ref, if_ref, qb_ref, zb_ref, ib_ref, lb_ref, s0_ref, of_ref, ob_ref, s_ref):
    c = pl.program_id(1)
    nbatch, L = qf_ref.shape[0], qf_ref.shape[1]

    @pl.when(c == 0)
    def _():
        s_ref[...] = s0_ref[...]

    row = lax.broadcasted_iota(jnp.int32, (L, L), 0)
    col = lax.broadcasted_iota(jnp.int32, (L, L), 1)
    for d, (q_ref, z_ref, i_ref, o_ref) in enumerate(((qf_ref, zf_ref, if_ref, of_ref),
                                                      (qb_ref, zb_ref, ib_ref, ob_ref))):
        keep = (col <= row) if d == 0 else (col >= row)
        tri = jnp.where(keep, 1.0, 0.0).astype(BF16)
        last = L - 1 if d == 0 else 0
        for bb in range(nbatch):
            for h in range(C_HEADS):
                hs = slice(h * C_DK, (h + 1) * C_DK)
                lbh = lb_ref[d:d + 1, hs]
                f = lbh + (1.0 - lbh) * _sigmoid(z_ref[bb, :, hs].astype(F32))
                key = 1.0 - f
                b = _cumsum_rows(tri, jnp.log(f))
                ref = b[L // 2:L // 2 + 1, :]
                bend = b[last:last + 1, :]
                q = q_ref[bb, :, hs].astype(F32)
                v = i_ref[bb, :, hs]
                att = _dot_nt((q * jnp.exp(b - ref)).astype(BF16), (key * jnp.exp(ref - b)).astype(BF16))
                att = jnp.where(keep, att, 0.0).astype(BF16)
                st = s_ref[bb, d * C_HEADS + h]
                o = _dot_nt((q * jnp.exp(b)).astype(BF16), st.astype(BF16)) + _dot(att, v)
                o_ref[bb, :, hs] = o
                kdec = (key * jnp.exp(bend - b)).astype(BF16)
                vt = v.astype(F32).T.astype(BF16)
                s_ref[bb, d * C_HEADS + h] = jnp.exp(bend) * st + _dot(vt, kdec)


def hgrn_scan(pp, lb2, s0):
    B, n, _ = pp.shape
    L = SCAN_CHUNK
    nc = n // L
    nbatch = HGRN_BATCH if B % HGRN_BATCH == 0 else 1
    fwd = lambda blk: pl.BlockSpec((nbatch, L, 512), lambda b, c: (b, c, blk))
    bwd = lambda blk: pl.BlockSpec((nbatch, L, 512), lambda b, c: (b, nc - 1 - c, blk))
    st = pl.BlockSpec((nbatch, 2 * C_HEADS, C_DV, C_DK), lambda b, c: (b, 0, 0, 0))
    oshape = jax.ShapeDtypeStruct((B, n, 512), F32)
    return pl.pallas_call(
        _hgrn_body, grid=(B // nbatch, nc),
        in_specs=[fwd(1), fwd(2), fwd(4), bwd(1), bwd(3), bwd(4),
                  pl.BlockSpec((2, 512), lambda b, c: (0, 0)), st],
        out_specs=[fwd(0), bwd(0), st],
        out_shape=[oshape, oshape, jax.ShapeDtypeStruct(s0.shape, F32)],
        compiler_params=_cparams(("parallel", "arbitrary")), name="hgrn_scan",
    )(pp, pp, pp, pp, pp, pp, lb2, s0)


def _place_half(x, src_half, dst_half):
    return x if src_half == dst_half else pltpu.roll(x, 64, 1)


def _window_body(sink_ref, q_ref, *refs, local):
    if local:
        band_ref, kp_ref, ko_ref, kn_ref, vp_ref, vo_ref, vn_ref, kc_ref, vc_ref, o_ref = refs
    else:
        kc_ref, vc_ref, o_ref = refs
    tq = q_ref.shape[0]
    G = D_HEADS // D_KV_HEADS
    if local:
        W = WINDOW
        k_loc = jnp.concatenate([kp_ref[...], ko_ref[...], kn_ref[...]], axis=0)
        v_loc = jnp.concatenate([vp_ref[...], vo_ref[...], vn_ref[...]], axis=0)
        band = band_ref[...]
        valid = jnp.concatenate([band] * G, axis=0) > 0.5
    kc, vc = kc_ref[...], vc_ref[...]
    lo = _half_mask((tq, LANES), 0)
    for kh in range(D_KV_HEADS):
        tiles, sinks = [], []
        for g in range(G):
            hq = kh * G + g
            qp = q_ref[:, (hq // 2) * LANES:(hq // 2 + 1) * LANES].astype(F32)
            q = _place_half(qp, hq % 2, kh)
            tiles.append((jnp.where(_half_mask(q.shape, kh), q, 0.0) * (D_HD ** -0.5)).astype(BF16))
            sinks.append(jnp.full((tq, 1), sink_ref[hq], F32))
        q = jnp.concatenate(tiles, axis=0)
        scores = []
        if local:
            scores.append(jnp.where(valid, _dot_nt(q, k_loc), MASK_VALUE))
        scores.append(_dot_nt(q, kc))
        es, rl = _softmax_parts(scores, extra=jnp.concatenate(sinks, axis=0))
        vals = [v_loc, vc] if local else [vc]
        r = None
        for e, v in zip(es, vals):
            t = _dot((e * rl).astype(BF16), v)
            r = t if r is None else r + t
        outs = [_place_half(r[g * tq:(g + 1) * tq], kh, g % 2) for g in range(G)]
        for sub in range(G // 2):
            pair = kh * (G // 2) + sub
            o_ref[:, pair * LANES:(pair + 1) * LANES] = jnp.where(lo, outs[2 * sub], outs[2 * sub + 1]).astype(o_ref.dtype)


def _band_masks(nb):
    W = WINDOW
    a = jnp.arange(W)[:, None]
    j = jnp.arange(3 * W)[None, :]
    near = jnp.abs(j - W - a) <= W
    masks = []
    for i in (0, 1, nb - 1) if nb > 1 else (0, 0, 0):
        kblk = i + j // W - 1
        masks.append(near & (kblk >= 0) & (kblk < nb))
    return jnp.stack(masks).astype(F32)


def window_attention(sink, pr_q, pp_x, pr_c, pp_c, local):
    B, nq, _ = pr_q.shape
    m = pr_c.shape[1]
    tq = WINDOW if local else min(nq, 256)
    nb = nq // tq
    in_specs = [pl.BlockSpec(memory_space=pltpu.SMEM),
                pl.BlockSpec((None, tq, 512), lambda b, i: (b, i, 2))]
    args = [sink, pr_q]
    if local:
        prev = lambda b, i: jnp.maximum(i - 1, 0)
        nxt = lambda b, i: jnp.minimum(i + 1, nb - 1)
        edge = lambda b, i: jnp.where(i == 0, 0, jnp.where(i == nb - 1, 2, 1))
        in_specs.append(pl.BlockSpec((None, tq, 3 * tq), lambda b, i: (edge(b, i), 0, 0)))
        args.append(_band_masks(nb))
        for blk, arr in ((12, pr_q), (59, pp_x)):
            in_specs += [pl.BlockSpec((None, tq, 128), lambda b, i, blk=blk: (b, prev(b, i), blk)),
                         pl.BlockSpec((None, tq, 128), lambda b, i, blk=blk: (b, i, blk)),
                         pl.BlockSpec((None, tq, 128), lambda b, i, blk=blk: (b, nxt(b, i), blk))]
            args += [arr, arr, arr]
    in_specs += [pl.BlockSpec((None, m, 128), lambda b, i: (b, 0, 12)),
                 pl.BlockSpec((None, m, 128), lambda b, i: (b, 0, 59))]
    args += [pr_c, pp_c]
    return pl.pallas_call(
        functools.partial(_window_body, local=local), grid=(B, nb), in_specs=in_specs,
        out_specs=pl.BlockSpec((None, tq, 512), lambda b, i: (b, i, 0)),
        out_shape=jax.ShapeDtypeStruct((B, nq, 512), BF16),
        compiler_params=_cparams(("parallel", "parallel")), name="window_attention",
    )(*args)


def _merge_body(oa_ref, ob_ref, of_ref, obk_ref, hg_ref, hng_ref, od_ref, g0_ref, g1_ref, g2_ref, g3_ref,
                wb_ref, wo_ref, x_ref, gate_ref, o_ref):
    hng = hng_ref[...]
    oc = of_ref[...] + obk_ref[...]
    parts = []
    for h in range(C_HEADS):
        hs = slice(h * C_DV, (h + 1) * C_DV)
        g = hg_ref[:, hs].astype(F32)
        parts.append((_rms(oc[:, hs], hng) * (g * _sigmoid(g))).astype(BF16))
    oh = jnp.concatenate(parts, axis=1)
    y = None
    branches = (oa_ref[...], ob_ref[...], oh, od_ref[...])
    for j, (br, gl_ref) in enumerate(zip(branches, (g0_ref, g1_ref, g2_ref, g3_ref))):
        term = _sigmoid(gl_ref[...].astype(F32)) * _dot(br, wb_ref[j])
        y = term if y is None else y + term
    o_ref[...] = x_ref[...] + gate_ref[...] * _dot(y.astype(BF16), wo_ref[...])


def merge(oa, ob, of, obk, pp, hgrn_g, od, w_branch, w_out, x, gate):
    B, n, D = x.shape
    tm = min(n, 256)
    blk = lambda w, k: pl.BlockSpec((None, tm, w), lambda b, i: (b, i, k))
    in_specs = [blk(512, 0), blk(512, 0), blk(512, 0), blk(512, 0), blk(512, 5),
                pl.BlockSpec((1, C_DV), lambda b, i: (0, 0)), blk(512, 0),
                blk(1024, 3), blk(1024, 4), blk(1024, 5), blk(1024, 6),
                pl.BlockSpec((N_BRANCH, BR_W, D), lambda b, i: (0, 0, 0)),
                pl.BlockSpec((D, D), lambda b, i: (0, 0)),
                blk(D, 0), pl.BlockSpec((None, 1, D), lambda b, i: (b, 0, 0))]
    return pl.pallas_call(
        _merge_body, grid=(B, n // tm), in_specs=in_specs, out_specs=blk(D, 0),
        out_shape=jax.ShapeDtypeStruct((B, n, D), F32),
        compiler_params=_cparams(("parallel", "parallel")), name="merge",
    )(oa, ob, of, obk, pp, hgrn_g.reshape(1, C_DV), od, pp, pp, pp, pp, w_branch, w_out, x, gate)


N_SORTED = PEER_TOPK
GROUP_KEYS = 2
PAIR_CANDS = [(a, b) for a in range(N_SORTED) for b in range(N_SORTED) if (a + 1) * (b + 1) <= N_SORTED]


def _peer_route_body(x_ref, g_ref, sh_ref, sc_ref, wqt_ref, kcat_ref,
                     ht_ref, cnt_ref, e1_ref, rk_ref, e2_ref, work_ref, sv_ref):
    tb = x_ref.shape[0]
    H, NK = PEER_HEADS, N_KEYS
    y = _rms(x_ref[...], g_ref[...])
    ht = (y * (1.0 + sc_ref[...]) + sh_ref[...]).T.astype(BF16)
    ht_ref[...] = ht
    qt = _dot(wqt_ref[...], ht).astype(BF16)
    s_all = _dot(kcat_ref[...], qt)
    s1 = s_all[0:NK * H].reshape(NK, H, tb)
    s2 = s_all[NK * H:2 * NK * H].reshape(NK, H, tb)
    s2h = s_all[2 * NK * H:3 * NK * H]

    work_ref[0] = s1
    work_ref[1] = s2

    def body(r, prev):
        nxt = []
        for p in range(2):
            w = work_ref[p]
            mx = jnp.max(jnp.where(w < prev[p][None], w, -jnp.inf), axis=0)
            sv_ref[p, r] = mx
            nxt.append(mx)
        return tuple(nxt)

    inf = jnp.full((H, tb), jnp.inf, F32)
    lax.fori_loop(0, N_SORTED, body, (inf, inf))

    sv1 = [sv_ref[0, a] for a in range(N_SORTED)]
    sv2 = [sv_ref[1, b] for b in range(N_SORTED)]
    cands = [sv1[a] + sv2[b] for a, b in PAIR_CANDS]
    cur = list(cands)
    for r in range(PEER_TOPK):
        tau = functools.reduce(jnp.maximum, cur)
        cur = [jnp.where(cc == tau, -jnp.inf, cc) for cc in cur]
    top = sv1[0] + sv2[0]
    chosen = [cc >= tau for cc in cands]
    z = functools.reduce(lambda u, w: u + w,
                         [jnp.where(m, jnp.exp(cc - top), 0.0) for m, cc in zip(chosen, cands)])
    rz = 1.0 / z
    cnt_a = [jnp.zeros((H, tb), F32) for _ in range(N_SORTED)]
    for (a, b), m in zip(PAIR_CANDS, chosen):
        cnt_a[a] = cnt_a[a] + jnp.where(m, 1.0, 0.0)
    cnt = jnp.zeros_like(s1)
    for a in range(N_SORTED):
        cnt = jnp.where(s1 == sv1[a][None], cnt_a[a][None], cnt)
    cnt_ref[...] = cnt.reshape(NK * H, tb)
    e1_ref[...] = (jnp.exp(s1 - sv1[0][None]) * rz[None]).reshape(NK * H, tb)
    for h in range(H):
        rows = slice(h * NK, (h + 1) * NK)
        sh = s2h[rows, :]
        rank = None
        for b in range(N_SORTED):
            above = jnp.where(sv2[b][h:h + 1, :] > sh, 1.0, 0.0)
            rank = above if rank is None else rank + above
        rk_ref[rows, :] = rank.astype(BF16)
        e2_ref[rows, :] = jnp.exp(sh - sv2[0][h:h + 1, :]).astype(BF16)


def peer_route(x, g, shift, scale, wqt, kcat):
    B, n, D = x.shape
    tb = min(n, 256)
    nt = n // tb
    N = B * n
    R = N_KEYS * PEER_HEADS
    xf = x.reshape(N, D)
    vec = pl.BlockSpec((None, 1, D), lambda t: (t // nt, 0, 0))
    col = pl.BlockSpec((R, tb), lambda t: (0, t))
    cshape = jax.ShapeDtypeStruct((R, N), F32)
    hshape = jax.ShapeDtypeStruct((R, N), BF16)
    return pl.pallas_call(
        _peer_route_body, grid=(N // tb,),
        in_specs=[pl.BlockSpec((tb, D), lambda t: (t, 0)), pl.BlockSpec((1, D), lambda t: (0, 0)), vec, vec,
                  pl.BlockSpec((D, D), lambda t: (0, 0)), pl.BlockSpec((3 * R, D), lambda t: (0, 0))],
        out_specs=[pl.BlockSpec((D, tb), lambda t: (0, t)), col, col, col, col],
        out_shape=[jax.ShapeDtypeStruct((D, N), BF16), cshape, cshape, hshape, hshape],
        scratch_shapes=[pltpu.VMEM((2, N_KEYS, PEER_HEADS, tb), F32),
                        pltpu.VMEM((2, N_SORTED, PEER_HEADS, tb), F32)],
        compiler_params=_cparams(("parallel",)), name="peer_route",
    )(xf, g.reshape(1, D), shift, scale, wqt, kcat)


def _gelu(x):
    return 0.5 * x * (1.0 + lax.erf(x * math.sqrt(0.5)))


def _peer_dense_body(ht_ref, u_ref, vt_ref, cnt_ref, e1_ref, rk_ref, e2_ref, x_ref, gate_ref, o_ref,
                     acc_ref, act_a, act_b, p_a, p_b, rk_s, e2_s, *, groups):
    e = pl.program_id(1)
    H, NK, GI = PEER_HEADS, N_KEYS, GROUP_KEYS
    tb = ht_ref.shape[1]
    acts, ps = (act_a, act_b), (p_a, p_b)

    @pl.when(e == 0)
    def _():
        acc_ref[...] = jnp.zeros_like(acc_ref)
        rk_s[...] = rk_ref[...]
        e2_s[...] = e2_ref[...]

    def mxu_act(g, slot):
        acts[slot][...] = _dot(u_ref[g], ht_ref[...])

    def mxu_out(g, slot):
        acc_ref[...] += _dot(vt_ref[g], ps[slot][...])

    def vpu(g, slot):
        i0 = (e * groups + g) * GI
        for c in range(tb // LANES):
            cs = slice(c * LANES, (c + 1) * LANES)
            rows8 = [pl.ds(pl.multiple_of((i0 + k) * H, SUBLANES), H) for k in range(GI)]
            cnt8 = [cnt_ref[r, cs] for r in rows8]
            e18 = [e1_ref[r, cs] for r in rows8]
            w = [None] * GI
            for h in range(H):
                hr = slice(h * NK, (h + 1) * NK)
                rkt, e2t = rk_s[hr, cs], e2_s[hr, cs]
                for k in range(GI):
                    thr = jnp.broadcast_to(cnt8[k][h:h + 1, :], (NK, LANES)).astype(BF16)
                    wgt = jnp.broadcast_to(e18[k][h:h + 1, :], (NK, LANES)).astype(BF16)
                    term = jnp.where(rkt < thr, e2t, jnp.zeros_like(e2t)) * wgt
                    w[k] = term if w[k] is None else w[k] + term
            for k in range(GI):
                rows = slice(k * NK, (k + 1) * NK)
                ps[slot][rows, cs] = _gelu(acts[slot][rows, cs]).astype(BF16) * w[k]

    mxu_act(0, 0)
    vpu(0, 0)
    mxu_act(1, 1)

    def pair(k, carry):
        g = 2 * k + 1
        vpu(g, 1)
        mxu_act(g + 1, 0)
        mxu_out(g - 1, 0)
        vpu(g + 1, 0)
        mxu_act(g + 2, 1)
        mxu_out(g, 1)
        return carry

    lax.fori_loop(0, (groups - 2) // 2, pair, 0)
    vpu(groups - 1, 1)
    mxu_out(groups - 2, 0)
    mxu_out(groups - 1, 1)

    @pl.when(e == pl.num_programs(1) - 1)
    def _():
        o_ref[...] = x_ref[...] + gate_ref[...] * acc_ref[...].T


def peer_dense(ht, u3, vt3, cnt, e1, rk, e2, x, gate):
    B, n, D = x.shape
    N = B * n
    tb = min(n, 512)
    nt = n // tb
    groups = 8
    ge = GROUP_KEYS * N_KEYS
    R = N_KEYS * PEER_HEADS
    col = pl.BlockSpec((R, tb), lambda t, e: (0, t))
    out = pl.pallas_call(
        functools.partial(_peer_dense_body, groups=groups), grid=(N // tb, u3.shape[0] // groups),
        in_specs=[pl.BlockSpec((D, tb), lambda t, e: (0, t)),
                  pl.BlockSpec((groups, ge, D), lambda t, e: (e, 0, 0)),
                  pl.BlockSpec((groups, D, ge), lambda t, e: (e, 0, 0)),
                  col, col, col, col,
                  pl.BlockSpec((tb, D), lambda t, e: (t, 0)),
                  pl.BlockSpec((None, 1, D), lambda t, e: (t // nt, 0, 0))],
        out_specs=pl.BlockSpec((tb, D), lambda t, e: (t, 0)),
        out_shape=jax.ShapeDtypeStruct((N, D), F32),
        scratch_shapes=[pltpu.VMEM((D, tb), F32), pltpu.VMEM((ge, tb), F32), pltpu.VMEM((ge, tb), F32),
                        pltpu.VMEM((ge, tb), BF16), pltpu.VMEM((ge, tb), BF16),
                        pltpu.VMEM((R, tb), BF16), pltpu.VMEM((R, tb), BF16)],
        compiler_params=_cparams(("parallel", "arbitrary")), name="peer_dense",
    )(ht, u3, vt3, cnt, e1, rk, e2, x.reshape(N, D), gate)
    return out.reshape(B, n, D)


def _rope_tables(n):
    rows = n // GRID_W
    r = jnp.repeat(jnp.arange(rows, dtype=F32), GRID_W)
    col = jnp.tile(jnp.arange(GRID_W, dtype=F32), rows)
    m = ROPE_DIM // 2
    freqs = ROPE_BASE ** (-2.0 * jnp.arange(m // 2, dtype=F32) / m)
    ar, ac = r[:, None] * freqs, col[:, None] * freqs
    cos = jnp.concatenate([jnp.cos(ar), jnp.cos(ar), jnp.cos(ac), jnp.cos(ac)], axis=-1)
    sin = jnp.concatenate([-jnp.sin(ar), jnp.sin(ar), -jnp.sin(ac), jnp.sin(ac)], axis=-1)
    return jnp.tile(cos, (1, 2)), jnp.tile(sin, (1, 2))


def _prep_w_in(w):
    sizes = (256, 256, 256, 256, 512, 256, 128, 64, 512, 512, 512, 512, 512, 512, 128, 128, 4096)
    offs = np.concatenate([[0], np.cumsum(sizes)])
    piece = lambda k: w[:, offs[k]:offs[k + 1]]
    zeros = jnp.zeros((w.shape[0], 64), w.dtype)
    w_rope = jnp.concatenate([piece(0), piece(1), piece(2), piece(3), piece(13), piece(14), piece(7), zeros], axis=1)
    w_plain = jnp.concatenate([piece(4), piece(8), piece(9), piece(10), piece(11), piece(12), piece(16),
                               piece(5), piece(6), piece(15)], axis=1)
    return w_rope.astype(BF16), w_plain.astype(BF16)


def _prep_mla(w_uq, w_ukv):
    wq = w_uq.reshape(B_Q_LORA, B_HEADS, B_NOPE + B_ROPE)
    wq = jnp.pad(wq, ((0, 0), (0, 0), (0, 256 - B_NOPE - B_ROPE))).reshape(B_Q_LORA, B_HEADS * 256)
    wkv = w_ukv.reshape(B_KV_LORA, B_HEADS, B_NOPE + B_V)
    wkv = jnp.concatenate([wkv[:, :, :B_NOPE].reshape(B_KV_LORA, -1), wkv[:, :, B_NOPE:].reshape(B_KV_LORA, -1)], axis=1)
    return wq.astype(BF16), wkv.astype(BF16)


def _prep_peer_keys(keys):
    H, NK, hd = PEER_HEADS, N_KEYS, PEER_DK // 2
    eye = jnp.eye(H, dtype=keys.dtype)

    def place(p, head_major):
        k = keys[:, p]
        sel = jnp.zeros((2,), keys.dtype).at[p].set(1.0)
        full = jnp.einsum('hkd,hg,p->khgpd', k, eye, sel)
        if head_major:
            full = full.transpose(1, 0, 2, 3, 4)
        return full.reshape(NK * H, H * 2 * hd)

    return jnp.concatenate([place(0, False), place(1, False), place(1, True)], axis=0).astype(BF16)


def kernel(x, c, ctx, c_ctx, w_mod, b_mod, norm1_g, norm2_g, w_in, diff_lam_q1, diff_lam_k1, diff_lam_q2,
           diff_lam_k2, diff_norm_g, mla_qnorm_g, mla_kvnorm_g, mla_w_uq, mla_w_ukv, hgrn_lb, hgrn_norm_g,
           win_sink, w_branch, w_out, peer_wq, peer_keys, peer_u, peer_v, final_g):
    B, n, D = x.shape
    depth = w_mod.shape[0]
    rope = _rope_tables(n)
    lb_p = jax.nn.softmax(hgrn_lb.astype(F32), axis=1)
    lb = jnp.cumsum(lb_p, axis=1) - lb_p[:, :1]
    c16 = jnp.concatenate([c, c_ctx[None], jnp.zeros((16 - B - 1, D), F32)], axis=0)
    xc = ctx
    for l in range(depth):
        last = l == depth - 1
        mod = mod_vectors(c16, w_mod[l].astype(BF16), b_mod[l])
        mx = [mod[:B, k * D:(k + 1) * D][:, None, :] for k in range(6)]
        mc = [jnp.broadcast_to(mod[B, k * D:(k + 1) * D][None, None, :], (B, 1, D)) for k in range(6)]
        w_rope, w_plain = _prep_w_in(w_in[l])
        w_uq, w_ukv = _prep_mla(mla_w_uq[l], mla_w_ukv[l])
        wb, wo = w_branch[l].astype(BF16), w_out[l].astype(BF16)

        hx = norm_mod(x, norm1_g[l], mx[0], mx[1])
        hc = norm_mod(xc, norm1_g[l], mc[0], mc[1])
        pr_x = project(hx, w_rope, 896, rope)
        pp_x = project(hx, w_plain, 1536)
        pr_c = project(hc, w_rope, 896)
        pp_c = project(hc, w_plain, 1536)

        lam_init = 0.8 - 0.6 * math.exp(-0.3 * l)
        lam = (jnp.exp(jnp.sum(diff_lam_q1[l] * diff_lam_k1[l])) - jnp.exp(jnp.sum(diff_lam_q2[l] * diff_lam_k2[l]))
               + lam_init).reshape(1).astype(F32)
        qb_x, kvb_x = mla_up(pp_x, mla_qnorm_g[l], mla_kvnorm_g[l], w_uq, w_ukv, rope)
        qb_c, kvb_c = mla_up(pp_c, mla_qnorm_g[l], mla_kvnorm_g[l], w_uq, w_ukv, None)
        lb2 = lb[:, l, :]
        s0 = jnp.zeros((B, 2 * C_HEADS, C_DV, C_DK), F32)
        of_c, ob_c, s_ctx = hgrn_scan(pp_c, lb2, s0)
        sink = win_sink[l].astype(F32)

        oa = diff_attention(lam, pr_x, pr_x, pp_x, pr_c, pp_c, diff_norm_g[l], 1.0 - lam_init, True)
        ob = mla_attention(qb_x, kvb_x, pr_x, kvb_c, pr_c, True)
        of_x, ob_x, _ = hgrn_scan(pp_x, lb2, s_ctx)
        od = window_attention(sink, pr_x, pp_x, pr_c, pp_c, True)
        x_new = merge(oa, ob, of_x, ob_x, pp_x, hgrn_norm_g[l], od, wb, wo, x, mx[2])
        if not last:
            oa_c = diff_attention(lam, pr_c, None, None, pr_c, pp_c, diff_norm_g[l], 1.0 - lam_init, False)
            ob_c2 = mla_attention(qb_c, None, None, kvb_c, pr_c, False)
            od_c = window_attention(sink, pr_c, None, pr_c, pp_c, False)
            xc = merge(oa_c, ob_c2, of_c, ob_c, pp_c, hgrn_norm_g[l], od_c, wb, wo, xc, mc[2])
        x = x_new

        wqt = peer_wq[l].T.astype(BF16)
        kcat = _prep_peer_keys(peer_keys[l])
        ge = GROUP_KEYS * N_KEYS
        u3 = peer_u[l].astype(BF16).reshape(-1, ge, D)
        vt3 = peer_v[l].astype(BF16).reshape(-1, ge, D).transpose(0, 2, 1)
        routed = peer_route(x, norm2_g[l], mx[3], mx[4], wqt, kcat)
        x = peer_dense(routed[0], u3, vt3, *routed[1:], x, mx[5])
        if not last:
            routed = peer_route(xc, norm2_g[l], mc[3], mc[4], wqt, kcat)
            xc = peer_dense(routed[0], u3, vt3, *routed[1:], xc, mc[5])
    return final_norm(x, final_g)
```

```python
import functools
import math

import jax
import jax.numpy as jnp
import numpy as np
from jax import lax
from jax.experimental import pallas as pl
from jax.experimental.pallas import tpu as pltpu

F32 = jnp.float32
BF16 = jnp.bfloat16

D_MODEL = 1024
GRID_W = 64
EPS = 1e-6
ROPE_BASE = 10000.0
ROPE_DIM = 64
MASK_VALUE = -1e30
A_HEADS, A_QK, A_V = 4, 64, 128
B_HEADS, B_Q_LORA, B_KV_LORA, B_NOPE, B_ROPE, B_V = 4, 256, 128, 128, 64, 128
C_HEADS, C_DK, C_DV = 4, 128, 128
D_HEADS, D_KV_HEADS, D_HD, WINDOW = 8, 2, 64, 128
N_BRANCH, BR_W = 4, 512
PEER_HEADS, N_KEYS, PEER_DK, PEER_TOPK = 8, 128, 128, 16

LANES = 128
SUBLANES = 8
VMEM_LIMIT = 56 * 1024 * 1024
SCAN_CHUNK = 64
HGRN_BATCH = 4

ROPE_COLS = 1792
PLAIN_COLS = 7680


def _cparams(sem):
    return pltpu.CompilerParams(dimension_semantics=sem, vmem_limit_bytes=VMEM_LIMIT)


def _dot(a, b):
    return jnp.dot(a, b, preferred_element_type=F32)


def _dot_nt(a, b):
    return lax.dot_general(a, b, (((1,), (1,)), ((), ())), preferred_element_type=F32)


def _sigmoid(x):
    return 1.0 / (1.0 + jnp.exp(-x))


def _rms(x, g):
    return x * lax.rsqrt(jnp.mean(x * x, axis=-1, keepdims=True) + EPS) * g


def _swap16(x):
    lane = lax.broadcasted_iota(jnp.int32, x.shape, 1)
    up = pltpu.roll(x, LANES - 16, 1)
    dn = pltpu.roll(x, 16, 1)
    return jnp.where((lane & 31) < 16, up, dn)


def _rope128(x, cos, sin):
    return x * cos + _swap16(x) * sin


def _norm_mod_body(x_ref, g_ref, sh_ref, sc_ref, o_ref):
    y = _rms(x_ref[...], g_ref[...])
    o_ref[...] = (y * (1.0 + sc_ref[...]) + sh_ref[...]).astype(o_ref.dtype)


def _norm_body(x_ref, g_ref, o_ref):
    o_ref[...] = _rms(x_ref[...], g_ref[...]).astype(o_ref.dtype)


def norm_mod(x, g, shift, scale):
    B, n, D = x.shape
    tm = min(n, 512)
    row = pl.BlockSpec((None, tm, D), lambda b, i: (b, i, 0))
    vec = pl.BlockSpec((None, 1, D), lambda b, i: (b, 0, 0))
    return pl.pallas_call(
        _norm_mod_body, grid=(B, n // tm),
        in_specs=[row, pl.BlockSpec((1, D), lambda b, i: (0, 0)), vec, vec],
        out_specs=row, out_shape=jax.ShapeDtypeStruct((B, n, D), BF16),
        compiler_params=_cparams(("parallel", "parallel")), name="norm_mod",
    )(x, g.reshape(1, D), shift, scale)


def final_norm(x, g):
    B, n, D = x.shape
    tm = min(n, 512)
    row = pl.BlockSpec((None, tm, D), lambda b, i: (b, i, 0))
    return pl.pallas_call(
        _norm_body, grid=(B, n // tm),
        in_specs=[row, pl.BlockSpec((1, D), lambda b, i: (0, 0))],
        out_specs=row, out_shape=jax.ShapeDtypeStruct((B, n, D), x.dtype),
        compiler_params=_cparams(("parallel", "parallel")), name="final_norm",
    )(x, g.reshape(1, D))


def _mod_body(c_ref, w_ref, b_ref, o_ref):
    c = c_ref[...]
    s = (c * _sigmoid(c)).astype(BF16)
    o_ref[...] = _dot(s, w_ref[...]) + b_ref[...]


def mod_vectors(c16, w, b):
    R, D = c16.shape
    N = w.shape[1]
    tn = 1536
    return pl.pallas_call(
        _mod_body, grid=(N // tn,),
        in_specs=[pl.BlockSpec((R, D), lambda j: (0, 0)),
                  pl.BlockSpec((D, tn), lambda j: (0, j)),
                  pl.BlockSpec((1, tn), lambda j: (0, j))],
        out_specs=pl.BlockSpec((R, tn), lambda j: (0, j)),
        out_shape=jax.ShapeDtypeStruct((R, N), F32),
        compiler_params=_cparams(("parallel",)), name="mod_vectors",
    )(c16, w, b.reshape(1, N))


def _proj_body(a_ref, w_ref, o_ref):
    o_ref[...] = _dot(a_ref[...], w_ref[...]).astype(o_ref.dtype)


def _proj_rope_body(a_ref, w_ref, cos_ref, sin_ref, o_ref):
    acc = _dot(a_ref[...], w_ref[...])
    cos, sin = cos_ref[...], sin_ref[...]
    for j in range(acc.shape[1] // LANES):
        blk = acc[:, j * LANES:(j + 1) * LANES]
        o_ref[:, j * LANES:(j + 1) * LANES] = _rope128(blk, cos, sin).astype(o_ref.dtype)


def project(h, w, tn, rope=None):
    B, n, K = h.shape
    N = w.shape[1]
    tm = min(n, 512)
    nt = n // tm
    a = h.reshape(B * n, K)
    in_specs = [pl.BlockSpec((tm, K), lambda j, i: (i, 0)),
                pl.BlockSpec((K, tn), lambda j, i: (0, j))]
    args = [a, w]
    body = _proj_body
    if rope is not None:
        tab = pl.BlockSpec((tm, LANES), lambda j, i: (i % nt, 0))
        in_specs += [tab, tab]
        args += list(rope)
        body = _proj_rope_body
    out = pl.pallas_call(
        body, grid=(N // tn, (B * n) // tm), in_specs=in_specs,
        out_specs=pl.BlockSpec((tm, tn), lambda j, i: (i, j)),
        out_shape=jax.ShapeDtypeStruct((B * n, N), BF16),
        compiler_params=_cparams(("parallel", "parallel")), name="project",
    )(*args)
    return out.reshape(B, n, N)


def _softmax_parts(scores, extra=None):
    m = scores[0].max(axis=-1, keepdims=True)
    for s in scores[1:]:
        m = jnp.maximum(m, s.max(axis=-1, keepdims=True))
    if extra is not None:
        m = jnp.maximum(m, extra)
    es = [jnp.exp(s - m) for s in scores]
    l = es[0].sum(axis=-1, keepdims=True)
    for e in es[1:]:
        l = l + e.sum(axis=-1, keepdims=True)
    if extra is not None:
        l = l + jnp.exp(extra - m)
    return es, 1.0 / l


def _half_mask(shape, half):
    lane = lax.broadcasted_iota(jnp.int32, shape, 1)
    return (lane < 64) if half == 0 else (lane >= 64)


def _diff_attn_body(lam_ref, *refs, with_x, post_scale):
    if with_x:
        q1_ref, q2_ref, k1x_ref, k2x_ref, vx_ref, k1c_ref, k2c_ref, vc_ref, g_ref, o_ref = refs
    else:
        q1_ref, q2_ref, k1c_ref, k2c_ref, vc_ref, g_ref, o_ref = refs
    lam = lam_ref[0]
    g = g_ref[...]
    scale = A_QK ** -0.5
    for h in range(A_HEADS):
        pair = slice((h // 2) * LANES, (h // 2 + 1) * LANES)
        vs = slice(h * A_V, (h + 1) * A_V)

        def probs(q_ref, kx_ref, kc_ref):
            q = q_ref[:, pair]
            q = jnp.where(_half_mask(q.shape, h % 2), q, jnp.zeros_like(q)) * scale
            scores = [_dot_nt(q, kx_ref[:, pair])] if with_x else []
            scores.append(_dot_nt(q, kc_ref[:, pair]))
            es, rl = _softmax_parts(scores)
            return es, rl

        if with_x:
            e1, rl1 = probs(q1_ref, k1x_ref, k1c_ref)
            e2, rl2 = probs(q2_ref, k2x_ref, k2c_ref)
        else:
            e1, rl1 = probs(q1_ref, None, k1c_ref)
            e2, rl2 = probs(q2_ref, None, k2c_ref)
        rl2 = rl2 * lam
        vals = [vx_ref[:, vs], vc_ref[:, vs]] if with_x else [vc_ref[:, vs]]
        o = None
        for a, b, v in zip(e1, e2, vals):
            p = (a * rl1 - b * rl2).astype(BF16)
            t = _dot(p, v)
            o = t if o is None else o + t
        o_ref[:, vs] = (_rms(o, g) * post_scale).astype(o_ref.dtype)


def diff_attention(lam, pr_q, pr_x, pp_x, pr_c, pp_c, norm_g, post_scale, with_x):
    B, nq, _ = pr_q.shape
    m = pr_c.shape[1]
    tq = min(nq, 256)
    qspec = lambda blk: pl.BlockSpec((None, tq, 256), lambda b, i: (b, i, blk))
    in_specs = [pl.BlockSpec(memory_space=pltpu.SMEM), qspec(0), qspec(1)]
    args = [lam, pr_q, pr_q]
    if with_x:
        n = pr_x.shape[1]
        in_specs += [pl.BlockSpec((None, n, 256), lambda b, i: (b, 0, 2)),
                     pl.BlockSpec((None, n, 256), lambda b, i: (b, 0, 3)),
                     pl.BlockSpec((None, n, 512), lambda b, i: (b, 0, 0))]
        args += [pr_x, pr_x, pp_x]
    in_specs += [pl.BlockSpec((None, m, 256), lambda b, i: (b, 0, 2)),
                 pl.BlockSpec((None, m, 256), lambda b, i: (b, 0, 3)),
                 pl.BlockSpec((None, m, 512), lambda b, i: (b, 0, 0)),
                 pl.BlockSpec((1, A_V), lambda b, i: (0, 0))]
    args += [pr_c, pr_c, pp_c, norm_g.reshape(1, A_V)]
    return pl.pallas_call(
        functools.partial(_diff_attn_body, with_x=with_x, post_scale=post_scale),
        grid=(B, nq // tq), in_specs=in_specs,
        out_specs=pl.BlockSpec((None, tq, 512), lambda b, i: (b, i, 0)),
        out_shape=jax.ShapeDtypeStruct((B, nq, 512), BF16),
        compiler_params=_cparams(("parallel", "parallel")), name="diff_attention",
    )(*args)


def _mla_up_body(cq_ref, ckv_ref, qg_ref, kvg_ref, wq_ref, wkv_ref, *refs, rope):
    if rope:
        cos_ref, sin_ref, q_ref, kv_ref = refs
    else:
        q_ref, kv_ref = refs
    cq = _rms(cq_ref[...].astype(F32), qg_ref[...]).astype(BF16)
    ckv = _rms(ckv_ref[...].astype(F32), kvg_ref[...]).astype(BF16)
    kv_ref[...] = _dot(ckv, wkv_ref[...]).astype(kv_ref.dtype)
    q = _dot(cq, wq_ref[...])
    for h in range(B_HEADS):
        nope = slice(h * 256, h * 256 + LANES)
        ropes = slice(h * 256 + LANES, (h + 1) * 256)
        q_ref[:, nope] = q[:, nope].astype(q_ref.dtype)
        blk = q[:, ropes]
        if rope:
            blk = _rope128(blk, cos_ref[...], sin_ref[...])
        q_ref[:, ropes] = blk.astype(q_ref.dtype)


def mla_up(pp, qn_g, kvn_g, w_uq, w_ukv, rope):
    B, n, _ = pp.shape
    tm = min(n, 512)
    in_specs = [pl.BlockSpec((None, tm, 256), lambda b, i: (b, i, 28)),
                pl.BlockSpec((None, tm, 128), lambda b, i: (b, i, 58)),
                pl.BlockSpec((1, 256), lambda b, i: (0, 0)),
                pl.BlockSpec((1, 128), lambda b, i: (0, 0)),
                pl.BlockSpec((256, 1024), lambda b, i: (0, 0)),
                pl.BlockSpec((128, 1024), lambda b, i: (0, 0))]
    args = [pp, pp, qn_g.reshape(1, 256), kvn_g.reshape(1, 128), w_uq, w_ukv]
    if rope is not None:
        tab = pl.BlockSpec((tm, LANES), lambda b, i: (i, 0))
        in_specs += [tab, tab]
        args += list(rope)
    out = pl.BlockSpec((None, tm, 1024), lambda b, i: (b, i, 0))
    shp = jax.ShapeDtypeStruct((B, n, 1024), BF16)
    return pl.pallas_call(
        functools.partial(_mla_up_body, rope=rope is not None), grid=(B, n // tm),
        in_specs=in_specs, out_specs=[out, out], out_shape=[shp, shp],
        compiler_params=_cparams(("parallel", "parallel")), name="mla_up",
    )(*args)


def _mla_attn_body(*refs, with_x):
    if with_x:
        q_ref, knx_ref, vx_ref, krx_ref, knc_ref, vc_ref, krc_ref, o_ref = refs
    else:
        q_ref, knc_ref, vc_ref, krc_ref, o_ref = refs
    scale = (B_NOPE + B_ROPE) ** -0.5
    for h in range(B_HEADS):
        hs = slice(h * LANES, (h + 1) * LANES)
        qn = q_ref[:, h * 256:h * 256 + LANES]
        qr = q_ref[:, h * 256 + LANES:(h + 1) * 256]
        scores = []
        if with_x:
            scores.append((_dot_nt(qn, knx_ref[:, hs]) + _dot_nt(qr, krx_ref[...])) * scale)
        scores.append((_dot_nt(qn, knc_ref[:, hs]) + _dot_nt(qr, krc_ref[...])) * scale)
        es, rl = _softmax_parts(scores)
        vals = [vx_ref[:, hs], vc_ref[:, hs]] if with_x else [vc_ref[:, hs]]
        o = None
        for e, v in zip(es, vals):
            t = _dot((e * rl).astype(BF16), v)
            o = t if o is None else o + t
        o_ref[:, hs] = o.astype(o_ref.dtype)


def mla_attention(qb, kvb_x, pr_x, kvb_c, pr_c, with_x):
    B, nq, _ = qb.shape
    m = kvb_c.shape[1]
    tq = min(nq, 256)
    in_specs = [pl.BlockSpec((None, tq, 1024), lambda b, i: (b, i, 0))]
    args = [qb]
    if with_x:
        n = kvb_x.shape[1]
        in_specs += [pl.BlockSpec((None, n, 512), lambda b, i: (b, 0, 0)),
                     pl.BlockSpec((None, n, 512), lambda b, i: (b, 0, 1)),
                     pl.BlockSpec((None, n, 128), lambda b, i: (b, 0, 13))]
        args += [kvb_x, kvb_x, pr_x]
    in_specs += [pl.BlockSpec((None, m, 512), lambda b, i: (b, 0, 0)),
                 pl.BlockSpec((None, m, 512), lambda b, i: (b, 0, 1)),
                 pl.BlockSpec((None, m, 128), lambda b, i: (b, 0, 13))]
    args += [kvb_c, kvb_c, pr_c]
    return pl.pallas_call(
        functools.partial(_mla_attn_body, with_x=with_x), grid=(B, nq // tq),
        in_specs=in_specs, out_specs=pl.BlockSpec((None, tq, 512), lambda b, i: (b, i, 0)),
        out_shape=jax.ShapeDtypeStruct((B, nq, 512), BF16),
        compiler_params=_cparams(("parallel", "parallel")), name="mla_attention",
    )(*args)


def _cumsum_rows(tri, g):
    g1 = g.astype(BF16)
    r1 = g - g1.astype(F32)
    g2 = r1.astype(BF16)
    g3 = (r1 - g2.astype(F32)).astype(BF16)
    return _dot(tri, g1) + _dot(tri, g2) + _dot(tri, g3)


def _hgrn_body(qf_ref, zf_ref, if_ref, qb_ref, zb_ref, ib_ref, lb_ref, s0_ref, of_ref, ob_ref, s_ref):
    c = pl.program_id(1)
    nbatch, L = qf_ref.shape[0], qf_ref.shape[1]

    @pl.when(c == 0)
    def _():
        s_ref[...] = s0_ref[...]

    row = lax.broadcasted_iota(jnp.int32, (L, L), 0)
    col = lax.broadcasted_iota(jnp.int32, (L, L), 1)
    for d, (q_ref, z_ref, i_ref, o_ref) in enumerate(((qf_ref, zf_ref, if_ref, of_ref),
                                                      (qb_ref, zb_ref, ib_ref, ob_ref))):
        keep = (col <= row) if d == 0 else (col >= row)
        tri = jnp.where(keep, 1.0, 0.0).astype(BF16)
        last = L - 1 if d == 0 else 0
        for bb in range(nbatch):
            for h in range(C_HEADS):
                hs = slice(h * C_DK, (h + 1) * C_DK)
                lbh = lb_ref[d:d + 1, hs]
                f = lbh + (1.0 - lbh) * _sigmoid(z_ref[bb, :, hs].astype(F32))
                key = 1.0 - f
                b = _cumsum_rows(tri, jnp.log(f))
                ref = b[L // 2:L // 2 + 1, :]
                bend = b[last:last + 1, :]
                q = q_ref[bb, :, hs].astype(F32)
                v = i_ref[bb, :, hs]
                att = _dot_nt((q * jnp.exp(b - ref)).astype(BF16), (key * jnp.exp(ref - b)).astype(BF16))
                att = jnp.where(keep, att, 0.0).astype(BF16)
                st = s_ref[bb, d * C_HEADS + h]
                o = _dot_nt((q * jnp.exp(b)).astype(BF16), st.astype(BF16)) + _dot(att, v)
                o_ref[bb, :, hs] = o
                kdec = (key * jnp.exp(bend - b)).astype(BF16)
                vt = v.astype(F32).T.astype(BF16)
                s_ref[bb, d * C_HEADS + h] = jnp.exp(bend) * st + _dot(vt, kdec)


def hgrn_scan(pp, lb2, s0):
    B, n, _ = pp.shape
    L = SCAN_CHUNK
    nc = n // L
    nbatch = HGRN_BATCH if B % HGRN_BATCH == 0 else 1
    fwd = lambda blk: pl.BlockSpec((nbatch, L, 512), lambda b, c: (b, c, blk))
    bwd = lambda blk: pl.BlockSpec((nbatch, L, 512), lambda b, c: (b, nc - 1 - c, blk))
    st = pl.BlockSpec((nbatch, 2 * C_HEADS, C_DV, C_DK), lambda b, c: (b, 0, 0, 0))
    oshape = jax.ShapeDtypeStruct((B, n, 512), F32)
    return pl.pallas_call(
        _hgrn_body, grid=(B // nbatch, nc),
        in_specs=[fwd(1), fwd(2), fwd(4), bwd(1), bwd(3), bwd(4),
                  pl.BlockSpec((2, 512), lambda b, c: (0, 0)), st],
        out_specs=[fwd(0), bwd(0), st],
        out_shape=[oshape, oshape, jax.ShapeDtypeStruct(s0.shape, F32)],
        compiler_params=_cparams(("parallel", "arbitrary")), name="hgrn_scan",
    )(pp, pp, pp, pp, pp, pp, lb2, s0)


def _place_half(x, src_half, dst_half):
    return x if src_half == dst_half else pltpu.roll(x, 64, 1)


def _window_body(sink_ref, q_ref, *refs, local):
    if local:
        band_ref, kp_ref, ko_ref, kn_ref, vp_ref, vo_ref, vn_ref, kc_ref, vc_ref, o_ref = refs
    else:
        kc_ref, vc_ref, o_ref = refs
    tq = q_ref.shape[0]
    G = D_HEADS // D_KV_HEADS
    if local:
        W = WINDOW
        k_loc = jnp.concatenate([kp_ref[...], ko_ref[...], kn_ref[...]], axis=0)
        v_loc = jnp.concatenate([vp_ref[...], vo_ref[...], vn_ref[...]], axis=0)
        band = band_ref[...]
        valid = jnp.concatenate([band] * G, axis=0) > 0.5
    kc, vc = kc_ref[...], vc_ref[...]
    lo = _half_mask((tq, LANES), 0)
    for kh in range(D_KV_HEADS):
        tiles, sinks = [], []
        for g in range(G):
            hq = kh * G + g
            qp = q_ref[:, (hq // 2) * LANES:(hq // 2 + 1) * LANES].astype(F32)
            q = _place_half(qp, hq % 2, kh)
            tiles.append((jnp.where(_half_mask(q.shape, kh), q, 0.0) * (D_HD ** -0.5)).astype(BF16))
            sinks.append(jnp.full((tq, 1), sink_ref[hq], F32))
        q = jnp.concatenate(tiles, axis=0)
        scores = []
        if local:
            scores.append(jnp.where(valid, _dot_nt(q, k_loc), MASK_VALUE))
        scores.append(_dot_nt(q, kc))
        es, rl = _softmax_parts(scores, extra=jnp.concatenate(sinks, axis=0))
        vals = [v_loc, vc] if local else [vc]
        r = None
        for e, v in zip(es, vals):
            t = _dot((e * rl).astype(BF16), v)
            r = t if r is None else r + t
        outs = [_place_half(r[g * tq:(g + 1) * tq], kh, g % 2) for g in range(G)]
        for sub in range(G // 2):
            pair = kh * (G // 2) + sub
            o_ref[:, pair * LANES:(pair + 1) * LANES] = jnp.where(lo, outs[2 * sub], outs[2 * sub + 1]).astype(o_ref.dtype)


def _band_masks(nb):
    W = WINDOW
    a = jnp.arange(W)[:, None]
    j = jnp.arange(3 * W)[None, :]
    near = jnp.abs(j - W - a) <= W
    masks = []
    for i in (0, 1, nb - 1) if nb > 1 else (0, 0, 0):
        kblk = i + j // W - 1
        masks.append(near & (kblk >= 0) & (kblk < nb))
    return jnp.stack(masks).astype(F32)


def window_attention(sink, pr_q, pp_x, pr_c, pp_c, local):
    B, nq, _ = pr_q.shape
    m = pr_c.shape[1]
    tq = WINDOW if local else min(nq, 256)
    nb = nq // tq
    in_specs = [pl.BlockSpec(memory_space=pltpu.SMEM),
                pl.BlockSpec((None, tq, 512), lambda b, i: (b, i, 2))]
    args = [sink, pr_q]
    if local:
        prev = lambda b, i: jnp.maximum(i - 1, 0)
        nxt = lambda b, i: jnp.minimum(i + 1, nb - 1)
        edge = lambda b, i: jnp.where(i == 0, 0, jnp.where(i == nb - 1, 2, 1))
        in_specs.append(pl.BlockSpec((None, tq, 3 * tq), lambda b, i: (edge(b, i), 0, 0)))
        args.append(_band_masks(nb))
        for blk, arr in ((12, pr_q), (59, pp_x)):
            in_specs += [pl.BlockSpec((None, tq, 128), lambda b, i, blk=blk: (b, prev(b, i), blk)),
                         pl.BlockSpec((None, tq, 128), lambda b, i, blk=blk: (b, i, blk)),
                         pl.BlockSpec((None, tq, 128), lambda b, i, blk=blk: (b, nxt(b, i), blk))]
            args += [arr, arr, arr]
    in_specs += [pl.BlockSpec((None, m, 128), lambda b, i: (b, 0, 12)),
                 pl.BlockSpec((None, m, 128), lambda b, i: (b, 0, 59))]
    args += [pr_c, pp_c]
    return pl.pallas_call(
        functools.partial(_window_body, local=local), grid=(B, nb), in_specs=in_specs,
        out_specs=pl.BlockSpec((None, tq, 512), lambda b, i: (b, i, 0)),
        out_shape=jax.ShapeDtypeStruct((B, nq, 512), BF16),
        compiler_params=_cparams(("parallel", "parallel")), name="window_attention",
    )(*args)


def _merge_body(oa_ref, ob_ref, of_ref, obk_ref, hg_ref, hng_ref, od_ref, g0_ref, g1_ref, g2_ref, g3_ref,
                wb_ref, wo_ref, x_ref, gate_ref, o_ref):
    hng = hng_ref[...]
    oc = of_ref[...] + obk_ref[...]
    parts = []
    for h in range(C_HEADS):
        hs = slice(h * C_DV, (h + 1) * C_DV)
        g = hg_ref[:, hs].astype(F32)
        parts.append((_rms(oc[:, hs], hng) * (g * _sigmoid(g))).astype(BF16))
    oh = jnp.concatenate(parts, axis=1)
    y = None
    branches = (oa_ref[...], ob_ref[...], oh, od_ref[...])
    for j, (br, gl_ref) in enumerate(zip(branches, (g0_ref, g1_ref, g2_ref, g3_ref))):
        term = _sigmoid(gl_ref[...].astype(F32)) * _dot(br, wb_ref[j])
        y = term if y is None else y + term
    o_ref[...] = x_ref[...] + gate_ref[...] * _dot(y.astype(BF16), wo_ref[...])


def merge(oa, ob, of, obk, pp, hgrn_g, od, w_branch, w_out, x, gate):
    B, n, D = x.shape
    tm = min(n, 512)
    blk = lambda w, k: pl.BlockSpec((None, tm, w), lambda b, i: (b, i, k))
    in_specs = [blk(512, 0), blk(512, 0), blk(512, 0), blk(512, 0), blk(512, 5),
                pl.BlockSpec((1, C_DV), lambda b, i: (0, 0)), blk(512, 0),
                blk(1024, 3), blk(1024, 4), blk(1024, 5), blk(1024, 6),
                pl.BlockSpec((N_BRANCH, BR_W, D), lambda b, i: (0, 0, 0)),
                pl.BlockSpec((D, D), lambda b, i: (0, 0)),
                blk(D, 0), pl.BlockSpec((None, 1, D), lambda b, i: (b, 0, 0))]
    return pl.pallas_call(
        _merge_body, grid=(B, n // tm), in_specs=in_specs, out_specs=blk(D, 0),
        out_shape=jax.ShapeDtypeStruct((B, n, D), F32),
        compiler_params=_cparams(("parallel", "parallel")), name="merge",
    )(oa, ob, of, obk, pp, hgrn_g.reshape(1, C_DV), od, pp, pp, pp, pp, w_branch, w_out, x, gate)


N_SORTED = PEER_TOPK
GROUP_KEYS = 2
PAIR_CANDS = [(a, b) for a in range(N_SORTED) for b in range(N_SORTED) if (a + 1) * (b + 1) <= N_SORTED]


def _peer_route_body(x_ref, g_ref, sh_ref, sc_ref, wqt_ref, kcat_ref,
                     ht_ref, cnt_ref, e1_ref, rk_ref, e2_ref, work_ref, sv_ref):
    tb = x_ref.shape[0]
    H, NK = PEER_HEADS, N_KEYS
    y = _rms(x_ref[...], g_ref[...])
    ht = (y * (1.0 + sc_ref[...]) + sh_ref[...]).T.astype(BF16)
    ht_ref[...] = ht
    qt = _dot(wqt_ref[...], ht).astype(BF16)
    s_all = _dot(kcat_ref[...], qt)
    s1 = s_all[0:NK * H].reshape(NK, H, tb)
    s2 = s_all[NK * H:2 * NK * H].reshape(NK, H, tb)
    s2h = s_all[2 * NK * H:3 * NK * H]

    work_ref[0] = s1
    work_ref[1] = s2

    def body(r, prev):
        nxt = []
        for p in range(2):
            w = work_ref[p]
            mx = jnp.max(jnp.where(w < prev[p][None], w, -jnp.inf), axis=0)
            sv_ref[p, r] = mx
            nxt.append(mx)
        return tuple(nxt)

    inf = jnp.full((H, tb), jnp.inf, F32)
    lax.fori_loop(0, N_SORTED, body, (inf, inf))

    sv1 = [sv_ref[0, a] for a in range(N_SORTED)]
    sv2 = [sv_ref[1, b] for b in range(N_SORTED)]
    cands = [sv1[a] + sv2[b] for a, b in PAIR_CANDS]
    cur = list(cands)
    for r in range(PEER_TOPK):
        tau = functools.reduce(jnp.maximum, cur)
        cur = [jnp.where(cc == tau, -jnp.inf, cc) for cc in cur]
    top = sv1[0] + sv2[0]
    chosen = [cc >= tau for cc in cands]
    z = functools.reduce(lambda u, w: u + w,
                         [jnp.where(m, jnp.exp(cc - top), 0.0) for m, cc in zip(chosen, cands)])
    rz = 1.0 / z
    cnt_a = [jnp.zeros((H, tb), F32) for _ in range(N_SORTED)]
    for (a, b), m in zip(PAIR_CANDS, chosen):
        cnt_a[a] = cnt_a[a] + jnp.where(m, 1.0, 0.0)
    cnt = jnp.zeros_like(s1)
    for a in range(N_SORTED):
        cnt = jnp.where(s1 == sv1[a][None], cnt_a[a][None], cnt)
    cnt_ref[...] = cnt.reshape(NK * H, tb)
    e1_ref[...] = (jnp.exp(s1 - sv1[0][None]) * rz[None]).reshape(NK * H, tb)
    for h in range(H):
        rows = slice(h * NK, (h + 1) * NK)
        sh = s2h[rows, :]
        rank = None
        for b in range(N_SORTED):
            above = jnp.where(sv2[b][h:h + 1, :] > sh, 1.0, 0.0)
            rank = above if rank is None else rank + above
        rk_ref[rows, :] = rank.astype(BF16)
        e2_ref[rows, :] = jnp.exp(sh - sv2[0][h:h + 1, :]).astype(BF16)


def peer_route(x, g, shift, scale, wqt, kcat):
    B, n, D = x.shape
    tb = min(n, 512)
    nt = n // tb
    N = B * n
    R = N_KEYS * PEER_HEADS
    xf = x.reshape(N, D)
    vec = pl.BlockSpec((None, 1, D), lambda t: (t // nt, 0, 0))
    col = pl.BlockSpec((R, tb), lambda t: (0, t))
    cshape = jax.ShapeDtypeStruct((R, N), F32)
    hshape = jax.ShapeDtypeStruct((R, N), BF16)
    return pl.pallas_call(
        _peer_route_body, grid=(N // tb,),
        in_specs=[pl.BlockSpec((tb, D), lambda t: (t, 0)), pl.BlockSpec((1, D), lambda t: (0, 0)), vec, vec,
                  pl.BlockSpec((D, D), lambda t: (0, 0)), pl.BlockSpec((3 * R, D), lambda t: (0, 0))],
        out_specs=[pl.BlockSpec((D, tb), lambda t: (0, t)), col, col, col, col],
        out_shape=[jax.ShapeDtypeStruct((D, N), BF16), cshape, cshape, hshape, hshape],
        scratch_shapes=[pltpu.VMEM((2, N_KEYS, PEER_HEADS, tb), F32),
                        pltpu.VMEM((2, N_SORTED, PEER_HEADS, tb), F32)],
        compiler_params=_cparams(("parallel",)), name="peer_route",
    )(xf, g.reshape(1, D), shift, scale, wqt, kcat)


def _gelu(x):
    return 0.5 * x * (1.0 + lax.erf(x * math.sqrt(0.5)))


def _peer_dense_body(ht_ref, u_ref, vt_ref, cnt_ref, e1_ref, rk_ref, e2_ref, x_ref, gate_ref, o_ref,
                     acc_ref, act_a, act_b, p_a, p_b, rk_s, e2_s, *, groups):
    e = pl.program_id(1)
    H, NK, GI = PEER_HEADS, N_KEYS, GROUP_KEYS
    tb = ht_ref.shape[1]
    acts, ps = (act_a, act_b), (p_a, p_b)

    @pl.when(e == 0)
    def _():
        acc_ref[...] = jnp.zeros_like(acc_ref)
        rk_s[...] = rk_ref[...]
        e2_s[...] = e2_ref[...]

    def mxu_act(g, slot):
        acts[slot][...] = _dot(u_ref[g], ht_ref[...])

    def mxu_out(g, slot):
        acc_ref[...] += _dot(vt_ref[g], ps[slot][...])

    def vpu(g, slot):
        i0 = (e * groups + g) * GI
        for c in range(tb // LANES):
            cs = slice(c * LANES, (c + 1) * LANES)
            rows8 = [pl.ds(pl.multiple_of((i0 + k) * H, SUBLANES), H) for k in range(GI)]
            cnt8 = [cnt_ref[r, cs] for r in rows8]
            e18 = [e1_ref[r, cs] for r in rows8]
            w = [None] * GI
            for h in range(H):
                hr = slice(h * NK, (h + 1) * NK)
                rkt, e2t = rk_s[hr, cs], e2_s[hr, cs]
                for k in range(GI):
                    thr = jnp.broadcast_to(cnt8[k][h:h + 1, :], (NK, LANES)).astype(BF16)
                    wgt = jnp.broadcast_to(e18[k][h:h + 1, :], (NK, LANES)).astype(BF16)
                    term = jnp.where(rkt < thr, e2t, jnp.zeros_like(e2t)) * wgt
                    w[k] = term if w[k] is None else w[k] + term
            for k in range(GI):
                rows = slice(k * NK, (k + 1) * NK)
                ps[slot][rows, cs] = _gelu(acts[slot][rows, cs]).astype(BF16) * w[k]

    mxu_act(0, 0)
    vpu(0, 0)
    mxu_act(1, 1)

    def pair(k, carry):
        g = 2 * k + 1
        vpu(g, 1)
        mxu_act(g + 1, 0)
        mxu_out(g - 1, 0)
        vpu(g + 1, 0)
        mxu_act(g + 2, 1)
        mxu_out(g, 1)
        return carry

    lax.fori_loop(0, (groups - 2) // 2, pair, 0)
    vpu(groups - 1, 1)
    mxu_out(groups - 2, 0)
    mxu_out(groups - 1, 1)

    @pl.when(e == pl.num_programs(1) - 1)
    def _():
        o_ref[...] = x_ref[...] + gate_ref[...] * acc_ref[...].T


def peer_dense(ht, u3, vt3, cnt, e1, rk, e2, x, gate):
    B, n, D = x.shape
    N = B * n
    tb = min(n, 512)
    nt = n // tb
    groups = 8
    ge = GROUP_KEYS * N_KEYS
    R = N_KEYS * PEER_HEADS
    col = pl.BlockSpec((R, tb), lambda t, e: (0, t))
    out = pl.pallas_call(
        functools.partial(_peer_dense_body, groups=groups), grid=(N // tb, u3.shape[0] // groups),
        in_specs=[pl.BlockSpec((D, tb), lambda t, e: (0, t)),
                  pl.BlockSpec((groups, ge, D), lambda t, e: (e, 0, 0)),
                  pl.BlockSpec((groups, D, ge), lambda t, e: (e, 0, 0)),
                  col, col, col, col,
                  pl.BlockSpec((tb, D), lambda t, e: (t, 0)),
                  pl.BlockSpec((None, 1, D), lambda t, e: (t // nt, 0, 0))],
        out_specs=pl.BlockSpec((tb, D), lambda t, e: (t, 0)),
        out_shape=jax.ShapeDtypeStruct((N, D), F32),
        scratch_shapes=[pltpu.VMEM((D, tb), F32), pltpu.VMEM((ge, tb), F32), pltpu.VMEM((ge, tb), F32),
                        pltpu.VMEM((ge, tb), BF16), pltpu.VMEM((ge, tb), BF16),
                        pltpu.VMEM((R, tb), BF16), pltpu.VMEM((R, tb), BF16)],
        compiler_params=_cparams(("parallel", "arbitrary")), name="peer_dense",
    )(ht, u3, vt3, cnt, e1, rk, e2, x.reshape(N, D), gate)
    return out.reshape(B, n, D)


def _rope_tables(n):
    rows = n // GRID_W
    r = jnp.repeat(jnp.arange(rows, dtype=F32), GRID_W)
    col = jnp.tile(jnp.arange(GRID_W, dtype=F32), rows)
    m = ROPE_DIM // 2
    freqs = ROPE_BASE ** (-2.0 * jnp.arange(m // 2, dtype=F32) / m)
    ar, ac = r[:, None] * freqs, col[:, None] * freqs
    cos = jnp.concatenate([jnp.cos(ar), jnp.cos(ar), jnp.cos(ac), jnp.cos(ac)], axis=-1)
    sin = jnp.concatenate([-jnp.sin(ar), jnp.sin(ar), -jnp.sin(ac), jnp.sin(ac)], axis=-1)
    return jnp.tile(cos, (1, 2)), jnp.tile(sin, (1, 2))


def _prep_w_in(w):
    sizes = (256, 256, 256, 256, 512, 256, 128, 64, 512, 512, 512, 512, 512, 512, 128, 128, 4096)
    offs = np.concatenate([[0], np.cumsum(sizes)])
    piece = lambda k: w[:, offs[k]:offs[k + 1]]
    zeros = jnp.zeros((w.shape[0], 64), w.dtype)
    w_rope = jnp.concatenate([piece(0), piece(1), piece(2), piece(3), piece(13), piece(14), piece(7), zeros], axis=1)
    w_plain = jnp.concatenate([piece(4), piece(8), piece(9), piece(10), piece(11), piece(12), piece(16),
                               piece(5), piece(6), piece(15)], axis=1)
    return w_rope.astype(BF16), w_plain.astype(BF16)


def _prep_mla(w_uq, w_ukv):
    wq = w_uq.reshape(B_Q_LORA, B_HEADS, B_NOPE + B_ROPE)
    wq = jnp.pad(wq, ((0, 0), (0, 0), (0, 256 - B_NOPE - B_ROPE))).reshape(B_Q_LORA, B_HEADS * 256)
    wkv = w_ukv.reshape(B_KV_LORA, B_HEADS, B_NOPE + B_V)
    wkv = jnp.concatenate([wkv[:, :, :B_NOPE].reshape(B_KV_LORA, -1), wkv[:, :, B_NOPE:].reshape(B_KV_LORA, -1)], axis=1)
    return wq.astype(BF16), wkv.astype(BF16)


def _prep_peer_keys(keys):
    H, NK, hd = PEER_HEADS, N_KEYS, PEER_DK // 2
    eye = jnp.eye(H, dtype=keys.dtype)

    def place(p, head_major):
        k = keys[:, p]
        sel = jnp.zeros((2,), keys.dtype).at[p].set(1.0)
        full = jnp.einsum('hkd,hg,p->khgpd', k, eye, sel)
        if head_major:
            full = full.transpose(1, 0, 2, 3, 4)
        return full.reshape(NK * H, H * 2 * hd)

    return jnp.concatenate([place(0, False), place(1, False), place(1, True)], axis=0).astype(BF16)


def kernel(x, c, ctx, c_ctx, w_mod, b_mod, norm1_g, norm2_g, w_in, diff_lam_q1, diff_lam_k1, diff_lam_q2,
           diff_lam_k2, diff_norm_g, mla_qnorm_g, mla_kvnorm_g, mla_w_uq, mla_w_ukv, hgrn_lb, hgrn_norm_g,
           win_sink, w_branch, w_out, peer_wq, peer_keys, peer_u, peer_v, final_g):
    B, n, D = x.shape
    depth = w_mod.shape[0]
    rope = _rope_tables(n)
    lb_p = jax.nn.softmax(hgrn_lb.astype(F32), axis=1)
    lb = jnp.cumsum(lb_p, axis=1) - lb_p[:, :1]
    c16 = jnp.concatenate([c, c_ctx[None], jnp.zeros((16 - B - 1, D), F32)], axis=0)
    xc = ctx
    for l in range(depth):
        last = l == depth - 1
        mod = mod_vectors(c16, w_mod[l].astype(BF16), b_mod[l])
        mx = [mod[:B, k * D:(k + 1) * D][:, None, :] for k in range(6)]
        mc = [jnp.broadcast_to(mod[B, k * D:(k + 1) * D][None, None, :], (B, 1, D)) for k in range(6)]
        w_rope, w_plain = _prep_w_in(w_in[l])
        w_uq, w_ukv = _prep_mla(mla_w_uq[l], mla_w_ukv[l])
        wb, wo = w_branch[l].astype(BF16), w_out[l].astype(BF16)

        hx = norm_mod(x, norm1_g[l], mx[0], mx[1])
        hc = norm_mod(xc, norm1_g[l], mc[0], mc[1])
        pr_x = project(hx, w_rope, 896, rope)
        pp_x = project(hx, w_plain, 1536)
        pr_c = project(hc, w_rope, 896)
        pp_c = project(hc, w_plain, 1536)

        lam_init = 0.8 - 0.6 * math.exp(-0.3 * l)
        lam = (jnp.exp(jnp.sum(diff_lam_q1[l] * diff_lam_k1[l])) - jnp.exp(jnp.sum(diff_lam_q2[l] * diff_lam_k2[l]))
               + lam_init).reshape(1).astype(F32)
        qb_x, kvb_x = mla_up(pp_x, mla_qnorm_g[l], mla_kvnorm_g[l], w_uq, w_ukv, rope)
        qb_c, kvb_c = mla_up(pp_c, mla_qnorm_g[l], mla_kvnorm_g[l], w_uq, w_ukv, None)
        lb2 = lb[:, l, :]
        s0 = jnp.zeros((B, 2 * C_HEADS, C_DV, C_DK), F32)
        of_c, ob_c, s_ctx = hgrn_scan(pp_c, lb2, s0)
        sink = win_sink[l].astype(F32)

        oa = diff_attention(lam, pr_x, pr_x, pp_x, pr_c, pp_c, diff_norm_g[l], 1.0 - lam_init, True)
        ob = mla_attention(qb_x, kvb_x, pr_x, kvb_c, pr_c, True)
        of_x, ob_x, _ = hgrn_scan(pp_x, lb2, s_ctx)
        od = window_attention(sink, pr_x, pp_x, pr_c, pp_c, True)
        x_new = merge(oa, ob, of_x, ob_x, pp_x, hgrn_norm_g[l], od, wb, wo, x, mx[2])
        if not last:
            oa_c = diff_attention(lam, pr_c, None, None, pr_c, pp_c, diff_norm_g[l], 1.0 - lam_init, False)
            ob_c2 = mla_attention(qb_c, None, None, kvb_c, pr_c, False)
            od_c = window_attention(sink, pr_c, None, pr_c, pp_c, False)
            xc = merge(oa_c, ob_c2, of_c, ob_c, pp_c, hgrn_norm_g[l], od_c, wb, wo, xc, mc[2])
        x = x_new

        wqt = peer_wq[l].T.astype(BF16)
        kcat = _prep_peer_keys(peer_keys[l])
        ge = GROUP_KEYS * N_KEYS
        u3 = peer_u[l].astype(BF16).reshape(-1, ge, D)
        vt3 = peer_v[l].astype(BF16).reshape(-1, ge, D).transpose(0, 2, 1)
        routed = peer_route(x, norm2_g[l], mx[3], mx[4], wqt, kcat)
        x = peer_dense(routed[0], u3, vt3, *routed[1:], x, mx[5])
        if not last:
            routed = peer_route(xc, norm2_g[l], mc[3], mc[4], wqt, kcat)
            xc = peer_dense(routed[0], u3, vt3, *routed[1:], xc, mc[5])
    return final_norm(x, final_g)
```

```python
import functools
import math

import jax
import jax.numpy as jnp
import numpy as np
from jax import lax
from jax.experimental import pallas as pl
from jax.experimental.pallas import tpu as pltpu

F32 = jnp.float32
BF16 = jnp.bfloat16

D_MODEL = 1024
GRID_W = 64
EPS = 1e-6
ROPE_BASE = 10000.0
ROPE_DIM = 64
MASK_VALUE = -1e30
A_HEADS, A_QK, A_V = 4, 64, 128
B_HEADS, B_Q_LORA, B_KV_LORA, B_NOPE, B_ROPE, B_V = 4, 256, 128, 128, 64, 128
C_HEADS, C_DK, C_DV = 4, 128, 128
D_HEADS, D_KV_HEADS, D_HD, WINDOW = 8, 2, 64, 128
N_BRANCH, BR_W = 4, 512
PEER_HEADS, N_KEYS, PEER_DK, PEER_TOPK = 8, 128, 128, 16

LANES = 128
SUBLANES = 8
VMEM_LIMIT = 56 * 1024 * 1024
SCAN_CHUNK = 64
HGRN_BATCH = 4

ROPE_COLS = 1792
PLAIN_COLS = 7680


def _cparams(sem):
    return pltpu.CompilerParams(dimension_semantics=sem, vmem_limit_bytes=VMEM_LIMIT)


def _dot(a, b):
    return jnp.dot(a, b, preferred_element_type=F32)


def _dot_nt(a, b):
    return lax.dot_general(a, b, (((1,), (1,)), ((), ())), preferred_element_type=F32)


def _sigmoid(x):
    return 1.0 / (1.0 + jnp.exp(-x))


def _rms(x, g):
    return x * lax.rsqrt(jnp.mean(x * x, axis=-1, keepdims=True) + EPS) * g


def _swap16(x):
    lane = lax.broadcasted_iota(jnp.int32, x.shape, 1)
    up = pltpu.roll(x, LANES - 16, 1)
    dn = pltpu.roll(x, 16, 1)
    return jnp.where((lane & 31) < 16, up, dn)


def _rope128(x, cos, sin):
    return x * cos + _swap16(x) * sin


def _norm_mod_body(x_ref, g_ref, sh_ref, sc_ref, o_ref):
    y = _rms(x_ref[...], g_ref[...])
    o_ref[...] = (y * (1.0 + sc_ref[...]) + sh_ref[...]).astype(o_ref.dtype)


def _norm_body(x_ref, g_ref, o_ref):
    o_ref[...] = _rms(x_ref[...], g_ref[...]).astype(o_ref.dtype)


def norm_mod(x, g, shift, scale):
    B, n, D = x.shape
    tm = min(n, 512)
    row = pl.BlockSpec((None, tm, D), lambda b, i: (b, i, 0))
    vec = pl.BlockSpec((None, 1, D), lambda b, i: (b, 0, 0))
    return pl.pallas_call(
        _norm_mod_body, grid=(B, n // tm),
        in_specs=[row, pl.BlockSpec((1, D), lambda b, i: (0, 0)), vec, vec],
        out_specs=row, out_shape=jax.ShapeDtypeStruct((B, n, D), BF16),
        compiler_params=_cparams(("parallel", "parallel")), name="norm_mod",
    )(x, g.reshape(1, D), shift, scale)


def final_norm(x, g):
    B, n, D = x.shape
    tm = min(n, 512)
    row = pl.BlockSpec((None, tm, D), lambda b, i: (b, i, 0))
    return pl.pallas_call(
        _norm_body, grid=(B, n // tm),
        in_specs=[row, pl.BlockSpec((1, D), lambda b, i: (0, 0))],
        out_specs=row, out_shape=jax.ShapeDtypeStruct((B, n, D), x.dtype),
        compiler_params=_cparams(("parallel", "parallel")), name="final_norm",
    )(x, g.reshape(1, D))


def _mod_body(c_ref, w_ref, b_ref, o_ref):
    c = c_ref[...]
    s = (c * _sigmoid(c)).astype(BF16)
    o_ref[...] = _dot(s, w_ref[...]) + b_ref[...]


def mod_vectors(c16, w, b):
    R, D = c16.shape
    N = w.shape[1]
    tn = 1536
    return pl.pallas_call(
        _mod_body, grid=(N // tn,),
        in_specs=[pl.BlockSpec((R, D), lambda j: (0, 0)),
                  pl.BlockSpec((D, tn), lambda j: (0, j)),
                  pl.BlockSpec((1, tn), lambda j: (0, j))],
        out_specs=pl.BlockSpec((R, tn), lambda j: (0, j)),
        out_shape=jax.ShapeDtypeStruct((R, N), F32),
        compiler_params=_cparams(("parallel",)), name="mod_vectors",
    )(c16, w, b.reshape(1, N))


def _proj_body(a_ref, w_ref, o_ref):
    o_ref[...] = _dot(a_ref[...], w_ref[...]).astype(o_ref.dtype)


def _proj_rope_body(a_ref, w_ref, cos_ref, sin_ref, o_ref):
    acc = _dot(a_ref[...], w_ref[...])
    cos, sin = cos_ref[...], sin_ref[...]
    for j in range(acc.shape[1] // LANES):
        blk = acc[:, j * LANES:(j + 1) * LANES]
        o_ref[:, j * LANES:(j + 1) * LANES] = _rope128(blk, cos, sin).astype(o_ref.dtype)


def project(h, w, tn, rope=None):
    B, n, K = h.shape
    N = w.shape[1]
    tm = min(n, 512)
    nt = n // tm
    a = h.reshape(B * n, K)
    in_specs = [pl.BlockSpec((tm, K), lambda j, i: (i, 0)),
                pl.BlockSpec((K, tn), lambda j, i: (0, j))]
    args = [a, w]
    body = _proj_body
    if rope is not None:
        tab = pl.BlockSpec((tm, LANES), lambda j, i: (i % nt, 0))
        in_specs += [tab, tab]
        args += list(rope)
        body = _proj_rope_body
    out = pl.pallas_call(
        body, grid=(N // tn, (B * n) // tm), in_specs=in_specs,
        out_specs=pl.BlockSpec((tm, tn), lambda j, i: (i, j)),
        out_shape=jax.ShapeDtypeStruct((B * n, N), BF16),
        compiler_params=_cparams(("parallel", "parallel")), name="project",
    )(*args)
    return out.reshape(B, n, N)


def _softmax_parts(scores, extra=None):
    m = scores[0].max(axis=-1, keepdims=True)
    for s in scores[1:]:
        m = jnp.maximum(m, s.max(axis=-1, keepdims=True))
    if extra is not None:
        m = jnp.maximum(m, extra)
    es = [jnp.exp(s - m) for s in scores]
    l = es[0].sum(axis=-1, keepdims=True)
    for e in es[1:]:
        l = l + e.sum(axis=-1, keepdims=True)
    if extra is not None:
        l = l + jnp.exp(extra - m)
    return es, 1.0 / l


def _half_mask(shape, half):
    lane = lax.broadcasted_iota(jnp.int32, shape, 1)
    return (lane < 64) if half == 0 else (lane >= 64)


def _diff_attn_body(lam_ref, *refs, with_x, post_scale):
    if with_x:
        q1_ref, q2_ref, k1x_ref, k2x_ref, vx_ref, k1c_ref, k2c_ref, vc_ref, g_ref, o_ref = refs
    else:
        q1_ref, q2_ref, k1c_ref, k2c_ref, vc_ref, g_ref, o_ref = refs
    lam = lam_ref[0]
    g = g_ref[...]
    scale = A_QK ** -0.5
    for h in range(A_HEADS):
        pair = slice((h // 2) * LANES, (h // 2 + 1) * LANES)
        vs = slice(h * A_V, (h + 1) * A_V)

        def probs(q_ref, kx_ref, kc_ref):
            q = q_ref[:, pair]
            q = jnp.where(_half_mask(q.shape, h % 2), q, jnp.zeros_like(q)) * scale
            scores = [_dot_nt(q, kx_ref[:, pair])] if with_x else []
            scores.append(_dot_nt(q, kc_ref[:, pair]))
            es, rl = _softmax_parts(scores)
            return es, rl

        if with_x:
            e1, rl1 = probs(q1_ref, k1x_ref, k1c_ref)
            e2, rl2 = probs(q2_ref, k2x_ref, k2c_ref)
        else:
            e1, rl1 = probs(q1_ref, None, k1c_ref)
            e2, rl2 = probs(q2_ref, None, k2c_ref)
        rl2 = rl2 * lam
        vals = [vx_ref[:, vs], vc_ref[:, vs]] if with_x else [vc_ref[:, vs]]
        o = None
        for a, b, v in zip(e1, e2, vals):
            p = (a * rl1 - b * rl2).astype(BF16)
            t = _dot(p, v)
            o = t if o is None else o + t
        o_ref[:, vs] = (_rms(o, g) * post_scale).astype(o_ref.dtype)


def diff_attention(lam, pr_q, pr_x, pp_x, pr_c, pp_c, norm_g, post_scale, with_x):
    B, nq, _ = pr_q.shape
    m = pr_c.shape[1]
    tq = min(nq, 256)
    qspec = lambda blk: pl.BlockSpec((None, tq, 256), lambda b, i: (b, i, blk))
    in_specs = [pl.BlockSpec(memory_space=pltpu.SMEM), qspec(0), qspec(1)]
    args = [lam, pr_q, pr_q]
    if with_x:
        n = pr_x.shape[1]
        in_specs += [pl.BlockSpec((None, n, 256), lambda b, i: (b, 0, 2)),
                     pl.BlockSpec((None, n, 256), lambda b, i: (b, 0, 3)),
                     pl.BlockSpec((None, n, 512), lambda b, i: (b, 0, 0))]
        args += [pr_x, pr_x, pp_x]
    in_specs += [pl.BlockSpec((None, m, 256), lambda b, i: (b, 0, 2)),
                 pl.BlockSpec((None, m, 256), lambda b, i: (b, 0, 3)),
                 pl.BlockSpec((None, m, 512), lambda b, i: (b, 0, 0)),
                 pl.BlockSpec((1, A_V), lambda b, i: (0, 0))]
    args += [pr_c, pr_c, pp_c, norm_g.reshape(1, A_V)]
    return pl.pallas_call(
        functools.partial(_diff_attn_body, with_x=with_x, post_scale=post_scale),
        grid=(B, nq // tq), in_specs=in_specs,
        out_specs=pl.BlockSpec((None, tq, 512), lambda b, i: (b, i, 0)),
        out_shape=jax.ShapeDtypeStruct((B, nq, 512), BF16),
        compiler_params=_cparams(("parallel", "parallel")), name="diff_attention",
    )(*args)


def _mla_up_body(cq_ref, ckv_ref, qg_ref, kvg_ref, wq_ref, wkv_ref, *refs, rope):
    if rope:
        cos_ref, sin_ref, q_ref, kv_ref = refs
    else:
        q_ref, kv_ref = refs
    cq = _rms(cq_ref[...].astype(F32), qg_ref[...]).astype(BF16)
    ckv = _rms(ckv_ref[...].astype(F32), kvg_ref[...]).astype(BF16)
    kv_ref[...] = _dot(ckv, wkv_ref[...]).astype(kv_ref.dtype)
    q = _dot(cq, wq_ref[...])
    for h in range(B_HEADS):
        nope = slice(h * 256, h * 256 + LANES)
        ropes = slice(h * 256 + LANES, (h + 1) * 256)
        q_ref[:, nope] = q[:, nope].astype(q_ref.dtype)
        blk = q[:, ropes]
        if rope:
            blk = _rope128(blk, cos_ref[...], sin_ref[...])
        q_ref[:, ropes] = blk.astype(q_ref.dtype)


def mla_up(pp, qn_g, kvn_g, w_uq, w_ukv, rope):
    B, n, _ = pp.shape
    tm = min(n, 512)
    in_specs = [pl.BlockSpec((None, tm, 256), lambda b, i: (b, i, 28)),
                pl.BlockSpec((None, tm, 128), lambda b, i: (b, i, 58)),
                pl.BlockSpec((1, 256), lambda b, i: (0, 0)),
                pl.BlockSpec((1, 128), lambda b, i: (0, 0)),
                pl.BlockSpec((256, 1024), lambda b, i: (0, 0)),
                pl.BlockSpec((128, 1024), lambda b, i: (0, 0))]
    args = [pp, pp, qn_g.reshape(1, 256), kvn_g.reshape(1, 128), w_uq, w_ukv]
    if rope is not None:
        tab = pl.BlockSpec((tm, LANES), lambda b, i: (i, 0))
        in_specs += [tab, tab]
        args += list(rope)
    out = pl.BlockSpec((None, tm, 1024), lambda b, i: (b, i, 0))
    shp = jax.ShapeDtypeStruct((B, n, 1024), BF16)
    return pl.pallas_call(
        functools.partial(_mla_up_body, rope=rope is not None), grid=(B, n // tm),
        in_specs=in_specs, out_specs=[out, out], out_shape=[shp, shp],
        compiler_params=_cparams(("parallel", "parallel")), name="mla_up",
    )(*args)


def _mla_attn_body(*refs, with_x):
    if with_x:
        q_ref, knx_ref, vx_ref, krx_ref, knc_ref, vc_ref, krc_ref, o_ref = refs
    else:
        q_ref, knc_ref, vc_ref, krc_ref, o_ref = refs
    scale = (B_NOPE + B_ROPE) ** -0.5
    for h in range(B_HEADS):
        hs = slice(h * LANES, (h + 1) * LANES)
        qn = q_ref[:, h * 256:h * 256 + LANES]
        qr = q_ref[:, h * 256 + LANES:(h + 1) * 256]
        scores = []
        if with_x:
            scores.append((_dot_nt(qn, knx_ref[:, hs]) + _dot_nt(qr, krx_ref[...])) * scale)
        scores.append((_dot_nt(qn, knc_ref[:, hs]) + _dot_nt(qr, krc_ref[...])) * scale)
        es, rl = _softmax_parts(scores)
        vals = [vx_ref[:, hs], vc_ref[:, hs]] if with_x else [vc_ref[:, hs]]
        o = None
        for e, v in zip(es, vals):
            t = _dot((e * rl).astype(BF16), v)
            o = t if o is None else o + t
        o_ref[:, hs] = o.astype(o_ref.dtype)


def mla_attention(qb, kvb_x, pr_x, kvb_c, pr_c, with_x):
    B, nq, _ = qb.shape
    m = kvb_c.shape[1]
    tq = min(nq, 256)
    in_specs = [pl.BlockSpec((None, tq, 1024), lambda b, i: (b, i, 0))]
    args = [qb]
    if with_x:
        n = kvb_x.shape[1]
        in_specs += [pl.BlockSpec((None, n, 512), lambda b, i: (b, 0, 0)),
                     pl.BlockSpec((None, n, 512), lambda b, i: (b, 0, 1)),
                     pl.BlockSpec((None, n, 128), lambda b, i: (b, 0, 13))]
        args += [kvb_x, kvb_x, pr_x]
    in_specs += [pl.BlockSpec((None, m, 512), lambda b, i: (b, 0, 0)),
                 pl.BlockSpec((None, m, 512), lambda b, i: (b, 0, 1)),
                 pl.BlockSpec((None, m, 128), lambda b, i: (b, 0, 13))]
    args += [kvb_c, kvb_c, pr_c]
    return pl.pallas_call(
        functools.partial(_mla_attn_body, with_x=with_x), grid=(B, nq // tq),
        in_specs=in_specs, out_specs=pl.BlockSpec((None, tq, 512), lambda b, i: (b, i, 0)),
        out_shape=jax.ShapeDtypeStruct((B, nq, 512), BF16),
        compiler_params=_cparams(("parallel", "parallel")), name="mla_attention",
    )(*args)


def _cumsum_rows(tri, g):
    g1 = g.astype(BF16)
    r1 = g - g1.astype(F32)
    g2 = r1.astype(BF16)
    g3 = (r1 - g2.astype(F32)).astype(BF16)
    return _dot(tri, g1) + _dot(tri, g2) + _dot(tri, g3)


def _hgrn_body(qf_ref, zf_ref, if_ref, qb_ref, zb_ref, ib_ref, lb_ref, s0_ref, of_ref, ob_ref, s_ref):
    c = pl.program_id(1)
    nbatch, L = qf_ref.shape[0], qf_ref.shape[1]

    @pl.when(c == 0)
    def _():
        s_ref[...] = s0_ref[...]

    row = lax.broadcasted_iota(jnp.int32, (L, L), 0)
    col = lax.broadcasted_iota(jnp.int32, (L, L), 1)
    for d, (q_ref, z_ref, i_ref, o_ref) in enumerate(((qf_ref, zf_ref, if_ref, of_ref),
                                                      (qb_ref, zb_ref, ib_ref, ob_ref))):
        keep = (col <= row) if d == 0 else (col >= row)
        tri = jnp.where(keep, 1.0, 0.0).astype(BF16)
        last = L - 1 if d == 0 else 0
        for bb in range(nbatch):
            for h in range(C_HEADS):
                hs = slice(h * C_DK, (h + 1) * C_DK)
                lbh = lb_ref[d:d + 1, hs]
                f = lbh + (1.0 - lbh) * _sigmoid(z_ref[bb, :, hs].astype(F32))
                key = 1.0 - f
                b = _cumsum_rows(tri, jnp.log(f))
                ref = b[L // 2:L // 2 + 1, :]
                bend = b[last:last + 1, :]
                q = q_ref[bb, :, hs].astype(F32)
                v = i_ref[bb, :, hs]
                att = _dot_nt((q * jnp.exp(b - ref)).astype(BF16), (key * jnp.exp(ref - b)).astype(BF16))
                att = jnp.where(keep, att, 0.0).astype(BF16)
                st = s_ref[bb, d * C_HEADS + h]
                o = _dot_nt((q * jnp.exp(b)).astype(BF16), st.astype(BF16)) + _dot(att, v)
                o_ref[bb, :, hs] = o
                kdec = (key * jnp.exp(bend - b)).astype(BF16)
                vt = v.astype(F32).T.astype(BF16)
                s_ref[bb, d * C_HEADS + h] = jnp.exp(bend) * st + _dot(vt, kdec)


def hgrn_scan(pp, lb2, s0):
    B, n, _ = pp.shape
    L = SCAN_CHUNK
    nc = n // L
    nbatch = HGRN_BATCH if B % HGRN_BATCH == 0 else 1
    fwd = lambda blk: pl.BlockSpec((nbatch, L, 512), lambda b, c: (b, c, blk))
    bwd = lambda blk: pl.BlockSpec((nbatch, L, 512), lambda b, c: (b, nc - 1 - c, blk))
    st = pl.BlockSpec((nbatch, 2 * C_HEADS, C_DV, C_DK), lambda b, c: (b, 0, 0, 0))
    oshape = jax.ShapeDtypeStruct((B, n, 512), F32)
    return pl.pallas_call(
        _hgrn_body, grid=(B // nbatch, nc),
        in_specs=[fwd(1), fwd(2), fwd(4), bwd(1), bwd(3), bwd(4),
                  pl.BlockSpec((2, 512), lambda b, c: (0, 0)), st],
        out_specs=[fwd(0), bwd(0), st],
        out_shape=[oshape, oshape, jax.ShapeDtypeStruct(s0.shape, F32)],
        compiler_params=_cparams(("parallel", "arbitrary")), name="hgrn_scan",
    )(pp, pp, pp, pp, pp, pp, lb2, s0)


def _place_half(x, src_half, dst_half):
    return x if src_half == dst_half else pltpu.roll(x, 64, 1)


def _window_body(sink_ref, q_ref, *refs, local):
    if local:
        band_ref, kp_ref, ko_ref, kn_ref, vp_ref, vo_ref, vn_ref, kc_ref, vc_ref, o_ref = refs
    else:
        kc_ref, vc_ref, o_ref = refs
    tq = q_ref.shape[0]
    G = D_HEADS // D_KV_HEADS
    if local:
        W = WINDOW
        k_loc = jnp.concatenate([kp_ref[...], ko_ref[...], kn_ref[...]], axis=0)
        v_loc = jnp.concatenate([vp_ref[...], vo_ref[...], vn_ref[...]], axis=0)
        band = band_ref[...]
        valid = jnp.concatenate([band] * G, axis=0) > 0.5
    kc, vc = kc_ref[...], vc_ref[...]
    lo = _half_mask((tq, LANES), 0)
    for kh in range(D_KV_HEADS):
        tiles, sinks = [], []
        for g in range(G):
            hq = kh * G + g
            qp = q_ref[:, (hq // 2) * LANES:(hq // 2 + 1) * LANES].astype(F32)
            q = _place_half(qp, hq % 2, kh)
            tiles.append((jnp.where(_half_mask(q.shape, kh), q, 0.0) * (D_HD ** -0.5)).astype(BF16))
            sinks.append(jnp.full((tq, 1), sink_ref[hq], F32))
        q = jnp.concatenate(tiles, axis=0)
        scores = []
        if local:
            scores.append(jnp.where(valid, _dot_nt(q, k_loc), MASK_VALUE))
        scores.append(_dot_nt(q, kc))
        es, rl = _softmax_parts(scores, extra=jnp.concatenate(sinks, axis=0))
        vals = [v_loc, vc] if local else [vc]
        r = None
        for e, v in zip(es, vals):
            t = _dot((e * rl).astype(BF16), v)
            r = t if r is None else r + t
        outs = [_place_half(r[g * tq:(g + 1) * tq], kh, g % 2) for g in range(G)]
        for sub in range(G // 2):
            pair = kh * (G // 2) + sub
            o_ref[:, pair * LANES:(pair + 1) * LANES] = jnp.where(lo, outs[2 * sub], outs[2 * sub + 1]).astype(o_ref.dtype)


def _band_masks(nb):
    W = WINDOW
    a = jnp.arange(W)[:, None]
    j = jnp.arange(3 * W)[None, :]
    near = jnp.abs(j - W - a) <= W
    masks = []
    for i in (0, 1, nb - 1) if nb > 1 else (0, 0, 0):
        kblk = i + j // W - 1
        masks.append(near & (kblk >= 0) & (kblk < nb))
    return jnp.stack(masks).astype(F32)


def window_attention(sink, pr_q, pp_x, pr_c, pp_c, local):
    B, nq, _ = pr_q.shape
    m = pr_c.shape[1]
    tq = WINDOW if local else min(nq, 256)
    nb = nq // tq
    in_specs = [pl.BlockSpec(memory_space=pltpu.SMEM),
                pl.BlockSpec((None, tq, 512), lambda b, i: (b, i, 2))]
    args = [sink, pr_q]
    if local:
        prev = lambda b, i: jnp.maximum(i - 1, 0)
        nxt = lambda b, i: jnp.minimum(i + 1, nb - 1)
        edge = lambda b, i: jnp.where(i == 0, 0, jnp.where(i == nb - 1, 2, 1))
        in_specs.append(pl.BlockSpec((None, tq, 3 * tq), lambda b, i: (edge(b, i), 0, 0)))
        args.append(_band_masks(nb))
        for blk, arr in ((12, pr_q), (59, pp_x)):
            in_specs += [pl.BlockSpec((None, tq, 128), lambda b, i, blk=blk: (b, prev(b, i), blk)),
                         pl.BlockSpec((None, tq, 128), lambda b, i, blk=blk: (b, i, blk)),
                         pl.BlockSpec((None, tq, 128), lambda b, i, blk=blk: (b, nxt(b, i), blk))]
            args += [arr, arr, arr]
    in_specs += [pl.BlockSpec((None, m, 128), lambda b, i: (b, 0, 12)),
                 pl.BlockSpec((None, m, 128), lambda b, i: (b, 0, 59))]
    args += [pr_c, pp_c]
    return pl.pallas_call(
        functools.partial(_window_body, local=local), grid=(B, nb), in_specs=in_specs,
        out_specs=pl.BlockSpec((None, tq, 512), lambda b, i: (b, i, 0)),
        out_shape=jax.ShapeDtypeStruct((B, nq, 512), BF16),
        compiler_params=_cparams(("parallel", "parallel")), name="window_attention",
    )(*args)


def _merge_body(oa_ref, ob_ref, of_ref, obk_ref, hg_ref, hng_ref, od_ref, g0_ref, g1_ref, g2_ref, g3_ref,
                wb_ref, wo_ref, x_ref, gate_ref, o_ref):
    hng = hng_ref[...]
    oc = of_ref[...] + obk_ref[...]
    parts = []
    for h in range(C_HEADS):
        hs = slice(h * C_DV, (h + 1) * C_DV)
        g = hg_ref[:, hs].astype(F32)
        parts.append((_rms(oc[:, hs], hng) * (g * _sigmoid(g))).astype(BF16))
    oh = jnp.concatenate(parts, axis=1)
    y = None
    branches = (oa_ref[...], ob_ref[...], oh, od_ref[...])
    for j, (br, gl_ref) in enumerate(zip(branches, (g0_ref, g1_ref, g2_ref, g3_ref))):
        term = _sigmoid(gl_ref[...].astype(F32)) * _dot(br, wb_ref[j])
        y = term if y is None else y + term
    o_ref[...] = x_ref[...] + gate_ref[...] * _dot(y.astype(BF16), wo_ref[...])


def merge(oa, ob, of, obk, pp, hgrn_g, od, w_branch, w_out, x, gate):
    B, n, D = x.shape
    tm = min(n, 512)
    blk = lambda w, k: pl.BlockSpec((None, tm, w), lambda b, i: (b, i, k))
    in_specs = [blk(512, 0), blk(512, 0), blk(512, 0), blk(512, 0), blk(512, 5),
                pl.BlockSpec((1, C_DV), lambda b, i: (0, 0)), blk(512, 0),
                blk(1024, 3), blk(1024, 4), blk(1024, 5), blk(1024, 6),
                pl.BlockSpec((N_BRANCH, BR_W, D), lambda b, i: (0, 0, 0)),
                pl.BlockSpec((D, D), lambda b, i: (0, 0)),
                blk(D, 0), pl.BlockSpec((None, 1, D), lambda b, i: (b, 0, 0))]
    return pl.pallas_call(
        _merge_body, grid=(B, n // tm), in_specs=in_specs, out_specs=blk(D, 0),
        out_shape=jax.ShapeDtypeStruct((B, n, D), F32),
        compiler_params=_cparams(("parallel", "parallel")), name="merge",
    )(oa, ob, of, obk, pp, hgrn_g.reshape(1, C_DV), od, pp, pp, pp, pp, w_branch, w_out, x, gate)


N_SORTED = PEER_TOPK
GROUP_KEYS = 2
PAIR_CANDS = [(a, b) for a in range(N_SORTED) for b in range(N_SORTED) if (a + 1) * (b + 1) <= N_SORTED]


def _peer_route_body(x_ref, g_ref, sh_ref, sc_ref, wqt_ref, kcat_ref,
                     ht_ref, cnt_ref, e1_ref, rk_ref, e2_ref, work_ref, sv_ref):
    tb = x_ref.shape[0]
    H, NK = PEER_HEADS, N_KEYS
    y = _rms(x_ref[...], g_ref[...])
    ht = (y * (1.0 + sc_ref[...]) + sh_ref[...]).T.astype(BF16)
    ht_ref[...] = ht
    qt = _dot(wqt_ref[...], ht).astype(BF16)
    s_all = _dot(kcat_ref[...], qt)
    s1 = s_all[0:NK * H].reshape(NK, H, tb)
    s2 = s_all[NK * H:2 * NK * H].reshape(NK, H, tb)
    s2h = s_all[2 * NK * H:3 * NK * H]

    work_ref[0] = s1
    work_ref[1] = s2

    def body(r, prev):
        nxt = []
        for p in range(2):
            w = work_ref[p]
            mx = jnp.max(jnp.where(w < prev[p][None], w, -jnp.inf), axis=0)
            sv_ref[p, r] = mx
            nxt.append(mx)
        return tuple(nxt)

    inf = jnp.full((H, tb), jnp.inf, F32)
    lax.fori_loop(0, N_SORTED, body, (inf, inf))

    sv1 = [sv_ref[0, a] for a in range(N_SORTED)]
    sv2 = [sv_ref[1, b] for b in range(N_SORTED)]
    cands = [sv1[a] + sv2[b] for a, b in PAIR_CANDS]
    cur = list(cands)
    for r in range(PEER_TOPK):
        tau = functools.reduce(jnp.maximum, cur)
        cur = [jnp.where(cc == tau, -jnp.inf, cc) for cc in cur]
    top = sv1[0] + sv2[0]
    chosen = [cc >= tau for cc in cands]
    z = functools.reduce(lambda u, w: u + w,
                         [jnp.where(m, jnp.exp(cc - top), 0.0) for m, cc in zip(chosen, cands)])
    rz = 1.0 / z
    cnt_a = [jnp.zeros((H, tb), F32) for _ in range(N_SORTED)]
    for (a, b), m in zip(PAIR_CANDS, chosen):
        cnt_a[a] = cnt_a[a] + jnp.where(m, 1.0, 0.0)
    cnt = jnp.zeros_like(s1)
    for a in range(N_SORTED):
        cnt = jnp.where(s1 == sv1[a][None], cnt_a[a][None], cnt)
    cnt_ref[...] = cnt.reshape(NK * H, tb)
    e1_ref[...] = (jnp.exp(s1 - sv1[0][None]) * rz[None]).reshape(NK * H, tb)
    for h in range(H):
        rows = slice(h * NK, (h + 1) * NK)
        sh = s2h[rows, :]
        rank = None
        for b in range(N_SORTED):
            above = jnp.where(sv2[b][h:h + 1, :] > sh, 1.0, 0.0)
            rank = above if rank is None else rank + above
        rk_ref[rows, :] = rank.astype(BF16)
        e2_ref[rows, :] = jnp.exp(sh - sv2[0][h:h + 1, :]).astype(BF16)


def peer_route(x, g, shift, scale, wqt, kcat):
    B, n, D = x.shape
    tb = min(n, 512)
    nt = n // tb
    N = B * n
    R = N_KEYS * PEER_HEADS
    xf = x.reshape(N, D)
    vec = pl.BlockSpec((None, 1, D), lambda t: (t // nt, 0, 0))
    col = pl.BlockSpec((R, tb), lambda t: (0, t))
    cshape = jax.ShapeDtypeStruct((R, N), F32)
    hshape = jax.ShapeDtypeStruct((R, N), BF16)
    return pl.pallas_call(
        _peer_route_body, grid=(N // tb,),
        in_specs=[pl.BlockSpec((tb, D), lambda t: (t, 0)), pl.BlockSpec((1, D), lambda t: (0, 0)), vec, vec,
                  pl.BlockSpec((D, D), lambda t: (0, 0)), pl.BlockSpec((3 * R, D), lambda t: (0, 0))],
        out_specs=[pl.BlockSpec((D, tb), lambda t: (0, t)), col, col, col, col],
        out_shape=[jax.ShapeDtypeStruct((D, N), BF16), cshape, cshape, hshape, hshape],
        scratch_shapes=[pltpu.VMEM((2, N_KEYS, PEER_HEADS, tb), F32),
                        pltpu.VMEM((2, N_SORTED, PEER_HEADS, tb), F32)],
        compiler_params=_cparams(("parallel",)), name="peer_route",
    )(xf, g.reshape(1, D), shift, scale, wqt, kcat)


def _gelu(x):
    return 0.5 * x * (1.0 + lax.erf(x * math.sqrt(0.5)))


def _peer_dense_body(ht_ref, u_ref, vt_ref, cnt_ref, e1_ref, rk_ref, e2_ref, x_ref, gate_ref, o_ref,
                     acc_ref, act_a, act_b, p_a, p_b, rk_s, e2_s, *, groups):
    e = pl.program_id(1)
    H, NK, GI = PEER_HEADS, N_KEYS, GROUP_KEYS
    tb = ht_ref.shape[1]
    acts, ps = (act_a, act_b), (p_a, p_b)

    @pl.when(e == 0)
    def _():
        acc_ref[...] = jnp.zeros_like(acc_ref)
        rk_s[...] = rk_ref[...]
        e2_s[...] = e2_ref[...]

    def mxu_act(g, slot):
        acts[slot][...] = _dot(u_ref[g], ht_ref[...])

    def mxu_out(g, slot):
        acc_ref[...] += _dot(vt_ref[g], ps[slot][...])

    def vpu(g, slot):
        i0 = (e * groups + g) * GI
        for c in range(tb // LANES):
            cs = slice(c * LANES, (c + 1) * LANES)
            rows8 = [pl.ds(pl.multiple_of((i0 + k) * H, SUBLANES), H) for k in range(GI)]
            cnt8 = [cnt_ref[r, cs] for r in rows8]
            e18 = [e1_ref[r, cs] for r in rows8]
            w = [None] * GI
            for h in range(H):
                hr = slice(h * NK, (h + 1) * NK)
                rkt, e2t = rk_s[hr, cs], e2_s[hr, cs]
                for k in range(GI):
                    thr = jnp.broadcast_to(cnt8[k][h:h + 1, :], (NK, LANES)).astype(BF16)
                    wgt = jnp.broadcast_to(e18[k][h:h + 1, :], (NK, LANES)).astype(BF16)
                    term = jnp.where(rkt < thr, e2t, jnp.zeros_like(e2t)) * wgt
                    w[k] = term if w[k] is None else w[k] + term
            for k in range(GI):
                rows = slice(k * NK, (k + 1) * NK)
                ps[slot][rows, cs] = _gelu(acts[slot][rows, cs]).astype(BF16) * w[k]

    mxu_act(0, 0)
    vpu(0, 0)
    mxu_act(1, 1)

    def pair(k, carry):
        g = 2 * k + 1
        vpu(g, 1)
        mxu_act(g + 1, 0)
        mxu_out(g - 1, 0)
        vpu(g + 1, 0)
        mxu_act(g + 2, 1)
        mxu_out(g, 1)
        return carry

    lax.fori_loop(0, (groups - 2) // 2, pair, 0)
    vpu(groups - 1, 1)
    mxu_out(groups - 2, 0)
    mxu_out(groups - 1, 1)

    @pl.when(e == pl.num_programs(1) - 1)
    def _():
        o_ref[...] = x_ref[...] + gate_ref[...] * acc_ref[...].T


def peer_dense(ht, u3, vt3, cnt, e1, rk, e2, x, gate):
    B, n, D = x.shape
    N = B * n
    tb = min(n, 1024)
    nt = n // tb
    groups = 8
    ge = GROUP_KEYS * N_KEYS
    R = N_KEYS * PEER_HEADS
    once = pl.Buffered(1)
    col = pl.BlockSpec((R, tb), lambda t, e: (0, t), pipeline_mode=once)
    out = pl.pallas_call(
        functools.partial(_peer_dense_body, groups=groups), grid=(N // tb, u3.shape[0] // groups),
        in_specs=[pl.BlockSpec((D, tb), lambda t, e: (0, t), pipeline_mode=once),
                  pl.BlockSpec((groups, ge, D), lambda t, e: (e, 0, 0)),
                  pl.BlockSpec((groups, D, ge), lambda t, e: (e, 0, 0)),
                  col, col, col, col,
                  pl.BlockSpec((tb, D), lambda t, e: (t, 0), pipeline_mode=once),
                  pl.BlockSpec((None, 1, D), lambda t, e: (t // nt, 0, 0))],
        out_specs=pl.BlockSpec((tb, D), lambda t, e: (t, 0)),
        out_shape=jax.ShapeDtypeStruct((N, D), F32),
        scratch_shapes=[pltpu.VMEM((D, tb), F32), pltpu.VMEM((ge, tb), F32), pltpu.VMEM((ge, tb), F32),
                        pltpu.VMEM((ge, tb), BF16), pltpu.VMEM((ge, tb), BF16),
                        pltpu.VMEM((R, tb), BF16), pltpu.VMEM((R, tb), BF16)],
        compiler_params=_cparams(("parallel", "arbitrary")), name="peer_dense",
    )(ht, u3, vt3, cnt, e1, rk, e2, x.reshape(N, D), gate)
    return out.reshape(B, n, D)


def _rope_tables(n):
    rows = n // GRID_W
    r = jnp.repeat(jnp.arange(rows, dtype=F32), GRID_W)
    col = jnp.tile(jnp.arange(GRID_W, dtype=F32), rows)
    m = ROPE_DIM // 2
    freqs = ROPE_BASE ** (-2.0 * jnp.arange(m // 2, dtype=F32) / m)
    ar, ac = r[:, None] * freqs, col[:, None] * freqs
    cos = jnp.concatenate([jnp.cos(ar), jnp.cos(ar), jnp.cos(ac), jnp.cos(ac)], axis=-1)
    sin = jnp.concatenate([-jnp.sin(ar), jnp.sin(ar), -jnp.sin(ac), jnp.sin(ac)], axis=-1)
    return jnp.tile(cos, (1, 2)), jnp.tile(sin, (1, 2))


def _prep_w_in(w):
    sizes = (256, 256, 256, 256, 512, 256, 128, 64, 512, 512, 512, 512, 512, 512, 128, 128, 4096)
    offs = np.concatenate([[0], np.cumsum(sizes)])
    piece = lambda k: w[:, offs[k]:offs[k + 1]]
    zeros = jnp.zeros((w.shape[0], 64), w.dtype)
    w_rope = jnp.concatenate([piece(0), piece(1), piece(2), piece(3), piece(13), piece(14), piece(7), zeros], axis=1)
    w_plain = jnp.concatenate([piece(4), piece(8), piece(9), piece(10), piece(11), piece(12), piece(16),
                               piece(5), piece(6), piece(15)], axis=1)
    return w_rope.astype(BF16), w_plain.astype(BF16)


def _prep_mla(w_uq, w_ukv):
    wq = w_uq.reshape(B_Q_LORA, B_HEADS, B_NOPE + B_ROPE)
    wq = jnp.pad(wq, ((0, 0), (0, 0), (0, 256 - B_NOPE - B_ROPE))).reshape(B_Q_LORA, B_HEADS * 256)
    wkv = w_ukv.reshape(B_KV_LORA, B_HEADS, B_NOPE + B_V)
    wkv = jnp.concatenate([wkv[:, :, :B_NOPE].reshape(B_KV_LORA, -1), wkv[:, :, B_NOPE:].reshape(B_KV_LORA, -1)], axis=1)
    return wq.astype(BF16), wkv.astype(BF16)


def _prep_peer_keys(keys):
    H, NK, hd = PEER_HEADS, N_KEYS, PEER_DK // 2
    eye = jnp.eye(H, dtype=keys.dtype)

    def place(p, head_major):
        k = keys[:, p]
        sel = jnp.zeros((2,), keys.dtype).at[p].set(1.0)
        full = jnp.einsum('hkd,hg,p->khgpd', k, eye, sel)
        if head_major:
            full = full.transpose(1, 0, 2, 3, 4)
        return full.reshape(NK * H, H * 2 * hd)

    return jnp.concatenate([place(0, False), place(1, False), place(1, True)], axis=0).astype(BF16)


def kernel(x, c, ctx, c_ctx, w_mod, b_mod, norm1_g, norm2_g, w_in, diff_lam_q1, diff_lam_k1, diff_lam_q2,
           diff_lam_k2, diff_norm_g, mla_qnorm_g, mla_kvnorm_g, mla_w_uq, mla_w_ukv, hgrn_lb, hgrn_norm_g,
           win_sink, w_branch, w_out, peer_wq, peer_keys, peer_u, peer_v, final_g):
    B, n, D = x.shape
    depth = w_mod.shape[0]
    rope = _rope_tables(n)
    lb_p = jax.nn.softmax(hgrn_lb.astype(F32), axis=1)
    lb = jnp.cumsum(lb_p, axis=1) - lb_p[:, :1]
    c16 = jnp.concatenate([c, c_ctx[None], jnp.zeros((16 - B - 1, D), F32)], axis=0)
    xc = ctx
    for l in range(depth):
        last = l == depth - 1
        mod = mod_vectors(c16, w_mod[l].astype(BF16), b_mod[l])
        mx = [mod[:B, k * D:(k + 1) * D][:, None, :] for k in range(6)]
        mc = [jnp.broadcast_to(mod[B, k * D:(k + 1) * D][None, None, :], (B, 1, D)) for k in range(6)]
        w_rope, w_plain = _prep_w_in(w_in[l])
        w_uq, w_ukv = _prep_mla(mla_w_uq[l], mla_w_ukv[l])
        wb, wo = w_branch[l].astype(BF16), w_out[l].astype(BF16)

        hx = norm_mod(x, norm1_g[l], mx[0], mx[1])
        hc = norm_mod(xc, norm1_g[l], mc[0], mc[1])
        pr_x = project(hx, w_rope, 896, rope)
        pp_x = project(hx, w_plain, 1536)
        pr_c = project(hc, w_rope, 896)
        pp_c = project(hc, w_plain, 1536)

        lam_init = 0.8 - 0.6 * math.exp(-0.3 * l)
        lam = (jnp.exp(jnp.sum(diff_lam_q1[l] * diff_lam_k1[l])) - jnp.exp(jnp.sum(diff_lam_q2[l] * diff_lam_k2[l]))
               + lam_init).reshape(1).astype(F32)
        qb_x, kvb_x = mla_up(pp_x, mla_qnorm_g[l], mla_kvnorm_g[l], w_uq, w_ukv, rope)
        qb_c, kvb_c = mla_up(pp_c, mla_qnorm_g[l], mla_kvnorm_g[l], w_uq, w_ukv, None)
        lb2 = lb[:, l, :]
        s0 = jnp.zeros((B, 2 * C_HEADS, C_DV, C_DK), F32)
        of_c, ob_c, s_ctx = hgrn_scan(pp_c, lb2, s0)
        sink = win_sink[l].astype(F32)

        oa = diff_attention(lam, pr_x, pr_x, pp_x, pr_c, pp_c, diff_norm_g[l], 1.0 - lam_init, True)
        ob = mla_attention(qb_x, kvb_x, pr_x, kvb_c, pr_c, True)
        of_x, ob_x, _ = hgrn_scan(pp_x, lb2, s_ctx)
        od = window_attention(sink, pr_x, pp_x, pr_c, pp_c, True)
        x_new = merge(oa, ob, of_x, ob_x, pp_x, hgrn_norm_g[l], od, wb, wo, x, mx[2])
        if not last:
            oa_c = diff_attention(lam, pr_c, None, None, pr_c, pp_c, diff_norm_g[l], 1.0 - lam_init, False)
            ob_c2 = mla_attention(qb_c, None, None, kvb_c, pr_c, False)
            od_c = window_attention(sink, pr_c, None, pr_c, pp_c, False)
            xc = merge(oa_c, ob_c2, of_c, ob_c, pp_c, hgrn_norm_g[l], od_c, wb, wo, xc, mc[2])
        x = x_new

        wqt = peer_wq[l].T.astype(BF16)
        kcat = _prep_peer_keys(peer_keys[l])
        ge = GROUP_KEYS * N_KEYS
        u3 = peer_u[l].astype(BF16).reshape(-1, ge, D)
        vt3 = peer_v[l].astype(BF16).reshape(-1, ge, D).transpose(0, 2, 1)
        routed = peer_route(x, norm2_g[l], mx[3], mx[4], wqt, kcat)
        x = peer_dense(routed[0], u3, vt3, *routed[1:], x, mx[5])
        if not last:
            routed = peer_route(xc, norm2_g[l], mc[3], mc[4], wqt, kcat)
            xc = peer_dense(routed[0], u3, vt3, *routed[1:], xc, mc[5])
    return final_norm(x, final_g)
```
